```python
import jax, jax.numpy as jnp
from jax import lax
import numpy as np

D_MODEL = 1024
BATCH = 1
SEQ = 16384
DEPTH = 2
DEC_BATCH = 32
DEC_SEQ = 1
PAST_LEN = 16384
PAGE_SIZE = 128

N_MIXERS = 4
W_GROUP = D_MODEL // N_MIXERS
D_MIX = N_MIXERS * W_GROUP
SC_WIDTH = 3
SSD_HEADS = 4
SSD_HEAD_DIM = W_GROUP // SSD_HEADS
SSD_GROUPS = 2
SSD_STATE = 128
SSD_CONV = 4
SSD_CHUNK = 128
SSD_CONV_DIM = W_GROUP + 2 * SSD_GROUPS * SSD_STATE
CF_WIDTH = 31
NSA_HEADS = 4
NSA_HEAD_DIM = W_GROUP // NSA_HEADS
CMP_STRIDE = 16
CMP_BLOCK = 2 * CMP_STRIDE
CMP_HIDDEN = 128
SEL_BLOCK = 64
SEL_TOPN = 16
SEL_LOCAL = 2
WINDOW = 512
Q_BLOCK = 128
KV_SLOTS = 4
NEG = -1e30
EPS = 1e-6

SPLITS = (W_GROUP, W_GROUP, W_GROUP, W_GROUP,
          W_GROUP, SSD_CONV_DIM, SSD_HEADS,
          2 * W_GROUP, W_GROUP,
          W_GROUP, 6 * NSA_HEAD_DIM, W_GROUP, 3 * NSA_HEADS)
D_IN = sum(SPLITS)

kernel_name = 'hymba_sconv_ssd_conformer_nsa_step'


def rmsnorm(x, g):
    x32 = x.astype(jnp.float32)
    y = x32 * lax.rsqrt(jnp.mean(x32 * x32, axis=-1, keepdims=True) + EPS)
    return y.astype(x.dtype) * g


def layernorm(x, g, b):
    x32 = x.astype(jnp.float32)
    mu = jnp.mean(x32, axis=-1, keepdims=True)
    var = jnp.mean(jnp.square(x32 - mu), axis=-1, keepdims=True)
    return ((x32 - mu) * lax.rsqrt(var + EPS)).astype(x.dtype) * g + b


def causal_dwconv(u, prev, w, b=None):
    full = jnp.concatenate([prev.astype(u.dtype), u], axis=1)
    y = lax.conv_general_dilated(full, w[:, None, :].astype(u.dtype), (1,), 'VALID',
                                 dimension_numbers=('NWC', 'WIO', 'NWC'),
                                 feature_group_count=u.shape[-1])
    if b is not None:
        y = y + b
    return y, full[:, full.shape[1] - (w.shape[0] - 1):]


def ssd_scan(x, dt, A, Bm, Cm, h0):
    f32 = jnp.float32
    n, l, hh, p = x.shape
    q = min(SSD_CHUNK, l)
    lp = -(-l // q) * q
    pad4 = ((0, 0), (0, lp - l), (0, 0), (0, 0))
    xf = jnp.pad(x.astype(f32), pad4)
    bf = jnp.pad(Bm.astype(f32), pad4)
    cf = jnp.pad(Cm.astype(f32), pad4)
    dtf = jnp.pad(dt, ((0, 0), (0, lp - l), (0, 0)))
    nc = lp // q
    xf = xf.reshape(n, nc, q, hh, p)
    bf = bf.reshape(n, nc, q, hh, -1)
    cf = cf.reshape(n, nc, q, hh, -1)
    dtf = dtf.reshape(n, nc, q, hh)
    acs = jnp.cumsum(jnp.moveaxis(dtf * A, 3, 1), axis=-1)
    causal = jnp.tril(jnp.ones((q, q), dtype=bool))
    decay = jnp.exp(jnp.where(causal, acs[..., :, None] - acs[..., None, :], -jnp.inf))
    xdt = xf * dtf[..., None]
    scores = jnp.einsum('nclhs,ncmhs->nhclm', cf, bf) * decay
    y_diag = jnp.einsum('nhclm,ncmhp->nclhp', scores, xdt)
    to_end = jnp.exp(acs[..., -1:] - acs)
    chunk_states = jnp.einsum('nclhs,nhcl,nclhp->nchps', bf, to_end, xdt)
    chunk_decay = jnp.exp(acs[..., -1])

    def step(h, inp):
        d, s = inp
        return d[:, :, None, None] * h + s, h

    h_fin, h_prev = lax.scan(step, h0.astype(f32),
                             (jnp.moveaxis(chunk_decay, 2, 0), jnp.moveaxis(chunk_states, 1, 0)))
    y_off = jnp.einsum('nclhs,cnhps,nhcl->nclhp', cf, h_prev, jnp.exp(acs))
    y = (y_diag + y_off).reshape(n, lp, hh, p)[:, :l]
    return y.astype(x.dtype), h_fin.astype(h0.dtype)


def masked_softmax(s, mask):
    p = jax.nn.softmax(jnp.where(mask, s, NEG), axis=-1)
    return jnp.where(mask, p, 0.0)


def compress_rows(rows, pe, w1, b1, w2, b2):
    n, s, d = rows.shape
    ch = rows.reshape(n, s // CMP_STRIDE, CMP_STRIDE, d)
    first = jnp.einsum('nmjd,jdf->nmf', ch, w1[:CMP_STRIDE])
    second = jnp.einsum('nmjd,jdf->nmf', ch, w1[CMP_STRIDE:])
    pe_term = jnp.einsum('jd,jdf->f', pe, w1)
    hid = jax.nn.silu(first[:, :-1] + second[:, 1:] + pe_term + b1)
    return hid @ w2 + b2


def compress_kv(k_raw, v_raw, lp):
    kc = compress_rows(k_raw, lp['cmp_pe'][0], lp['cmp_w1'][0], lp['cmp_b1'][0], lp['cmp_w2'][0], lp['cmp_b2'][0])
    vc = compress_rows(v_raw, lp['cmp_pe'][1], lp['cmp_w1'][1], lp['cmp_b1'][1], lp['cmp_w2'][1], lp['cmp_b2'][1])
    return rmsnorm(kc, lp['k_g'][0]), vc


def block_importance(pg, n_blocks):
    r = SEL_BLOCK // CMP_STRIDE
    nb = CMP_BLOCK // CMP_STRIDE - 1
    nc = pg.shape[-1]
    p = jnp.pad(pg, ((0, 0), (0, 0), (nb, r * n_blocks - nc)))
    total = None
    for o in range(-nb, r):
        w = (min(CMP_STRIDE * o + CMP_BLOCK, SEL_BLOCK) - max(CMP_STRIDE * o, 0)) / CMP_BLOCK
        start = o + nb
        term = w * p[..., start:start + r * (n_blocks - 1) + 1:r]
        total = term if total is None else total + term
    return total


def nsa_core(q, qpos, kc, vc, ks, vs, kw, vw, kpos, gates):
    n, tq = q.shape[:2]
    scale = NSA_HEAD_DIM ** -0.5
    nc = kc.shape[1]
    cvalid = (jnp.arange(nc) * CMP_STRIDE + CMP_BLOCK - 1)[None, :] <= qpos[:, None]
    s_c = jnp.einsum('nqhd,ncd->nqhc', q, kc).astype(jnp.float32) * scale
    p_c = masked_softmax(s_c, cvalid[None, :, None, :])
    o_c = jnp.einsum('nqhc,ncd->nqhd', p_c.astype(vc.dtype), vc)
    ns = ks.shape[1] // SEL_BLOCK
    p_sel = block_importance(p_c.sum(axis=2), ns)
    cur = qpos // SEL_BLOCK
    j = jnp.arange(ns)
    valid = j[None, :] <= cur[:, None]
    forced = valid & ((j[None, :] == 0) | (j[None, :] > cur[:, None] - SEL_LOCAL))
    score = jnp.where(forced, jnp.inf, jnp.where(valid, p_sel, -jnp.inf))
    n_sel = min(SEL_TOPN, ns)
    _, idx = lax.top_k(score, n_sel)
    tok = (idx[..., None] * SEL_BLOCK + jnp.arange(SEL_BLOCK)).reshape(n, tq, n_sel * SEL_BLOCK)
    ksel = jax.vmap(lambda k, t: k[t])(ks, tok)
    vsel = jax.vmap(lambda v, t: v[t])(vs, tok)
    smask = (tok <= qpos[None, :, None])[:, :, None, :]
    s_s = jnp.einsum('nqhd,nqkd->nqhk', q, ksel).astype(jnp.float32) * scale
    p_s = masked_softmax(s_s, smask)
    o_s = jnp.einsum('nqhk,nqkd->nqhd', p_s.astype(vsel.dtype), vsel)
    dpos = qpos[:, None] - kpos[None, :]
    wmask = (dpos >= 0) & (dpos <= WINDOW) & (kpos[None, :] >= 0)
    s_w = jnp.einsum('nqhd,nkd->nqhk', q, kw).astype(jnp.float32) * scale
    p_w = masked_softmax(s_w, wmask[None, :, None, :])
    o_w = jnp.einsum('nqhk,nkd->nqhd', p_w.astype(vw.dtype), vw)
    return gates[..., 0:1] * o_c + gates[..., 1:2] * o_s + gates[..., 2:3] * o_w


def attend_prompt(q, rows, win_rows, gates, lp):
    n, l = q.shape[:2]
    kc, vc = compress_kv(rows[:, :, 0], rows[:, :, 1], lp)
    ks, vs = rows[:, :, 2], rows[:, :, 3]
    kw_pad = jnp.pad(win_rows, ((0, 0), (WINDOW, 0), (0, 0), (0, 0)))

    def block(b):
        s0 = b * Q_BLOCK
        qb = lax.dynamic_slice_in_dim(q, s0, Q_BLOCK, axis=1)
        gb = lax.dynamic_slice_in_dim(gates, s0, Q_BLOCK, axis=1)
        kwb = lax.dynamic_slice_in_dim(kw_pad, s0, WINDOW + Q_BLOCK, axis=1)
        qpos = s0 + jnp.arange(Q_BLOCK)
        kpos = s0 - WINDOW + jnp.arange(WINDOW + Q_BLOCK)
        return nsa_core(qb, qpos, kc, vc, ks, vs, kwb[:, :, 0], kwb[:, :, 1], kpos, gb)

    o = lax.map(block, jnp.arange(l // Q_BLOCK))
    o = jnp.moveaxis(o, 0, 1).reshape(n, l, NSA_HEADS, NSA_HEAD_DIM)
    new_pages = rows.reshape(n * l // PAGE_SIZE, PAGE_SIZE, KV_SLOTS, NSA_HEAD_DIM)
    new_win = win_rows[:, l - min(WINDOW, l):]
    return o, (new_pages, new_win)


def make_attend_sample(cache_kv_l, cache_win_l, page_table):
    def attend(q, rows, win_rows, gates, lp):
        n, l = q.shape[:2]
        past = cache_kv_l[page_table].reshape(n, -1, KV_SLOTS, NSA_HEAD_DIM)
        p_len = past.shape[1]
        full = jnp.concatenate([past.astype(rows.dtype), rows], axis=1)
        t = p_len + l
        s = -(-t // SEL_BLOCK) * SEL_BLOCK
        full = jnp.pad(full, ((0, 0), (0, s - t), (0, 0), (0, 0)))
        kc, vc = compress_kv(full[:, :, 0], full[:, :, 1], lp)
        kw = jnp.concatenate([cache_win_l.astype(win_rows.dtype), win_rows], axis=1)
        wb = cache_win_l.shape[1]
        qpos = p_len + jnp.arange(l)
        kpos = p_len - wb + jnp.arange(wb + l)
        o = nsa_core(q, qpos, kc, vc, full[:, :, 2], full[:, :, 3], kw[:, :, 0], kw[:, :, 1], kpos, gates)
        return o, (rows, kw[:, l:])
    return attend


def trunk_layer(x, c, lp, prev_sc, prev_bconv, h0, prev_cf, attend):
    n, l, _ = x.shape
    ada = jax.nn.silu(c) @ lp['w_ada'] + lp['b_ada']
    shift, scale, ada_gate = jnp.split(ada, 3, axis=-1)
    h = rmsnorm(x, lp['norm_g']) * (1 + scale[:, None]) + shift[:, None]
    proj = h @ lp['w_in']
    offs = np.cumsum(SPLITS)[:-1].tolist()
    (a_h, a_b, a_c, a_g, b_z, b_xbc, b_dt, c_glu, c_g, d_q, d_kv, d_g, d_bg) = jnp.split(proj, offs, axis=-1)
    u_a, st_sc = causal_dwconv(a_c * a_h, prev_sc, lp['sc_w'])
    y_a = jax.nn.silu(a_g) * a_b * u_a
    xbc, st_bconv = causal_dwconv(b_xbc, prev_bconv, lp['ssd_conv_w'], lp['ssd_conv_b'])
    xbc = jax.nn.silu(xbc)
    xs, bs, cs = jnp.split(xbc, [W_GROUP, W_GROUP + SSD_GROUPS * SSD_STATE], axis=-1)
    rep = SSD_HEADS // SSD_GROUPS
    xs = xs.reshape(n, l, SSD_HEADS, SSD_HEAD_DIM)
    bs = jnp.repeat(bs.reshape(n, l, SSD_GROUPS, SSD_STATE), rep, axis=2)
    cs = jnp.repeat(cs.reshape(n, l, SSD_GROUPS, SSD_STATE), rep, axis=2)
    dt = jax.nn.softplus((b_dt + lp['ssd_dt_bias']).astype(jnp.float32))
    a_neg = -jnp.exp(lp['ssd_A_log'].astype(jnp.float32))
    y_ssd, st_h = ssd_scan(xs, dt, a_neg, bs, cs, h0)
    y_b = (y_ssd + lp['ssd_D'][:, None] * xs).reshape(n, l, W_GROUP)
    y_b = rmsnorm(y_b * jax.nn.silu(b_z), lp['ssd_norm_g'])
    glu_a, glu_b = jnp.split(c_glu, 2, axis=-1)
    u_c, st_cf = causal_dwconv(glu_a * jax.nn.sigmoid(glu_b), prev_cf, lp['cf_w'], lp['cf_b'])
    y_c = jax.nn.silu(c_g) * jax.nn.silu(layernorm(u_c, lp['cf_ln_g'], lp['cf_ln_b']))
    q = rmsnorm(d_q.reshape(n, l, NSA_HEADS, NSA_HEAD_DIM), lp['q_g'])
    kv = d_kv.reshape(n, l, 6, NSA_HEAD_DIM)
    k_slc = rmsnorm(kv[:, :, 2], lp['k_g'][1])
    k_win = rmsnorm(kv[:, :, 4], lp['k_g'][2])
    rows = jnp.stack([kv[:, :, 0], kv[:, :, 1], k_slc, kv[:, :, 3]], axis=2)
    win_rows = jnp.stack([k_win, kv[:, :, 5]], axis=2)
    gates = jax.nn.sigmoid(d_bg.reshape(n, l, NSA_HEADS, 3))
    o, st_attn = attend(q, rows, win_rows, gates, lp)
    y_d = jax.nn.silu(d_g) * o.reshape(n, l, W_GROUP)
    out = jnp.concatenate([y_a, y_b, y_c, y_d], axis=-1) @ lp['w_out']
    x = x + ada_gate[:, None] * out
    return x, st_attn + (st_sc, st_bconv, st_h, st_cf)


def setup_inputs(seed: int = 0) -> dict:
    key = jax.random.key(seed)
    keys = iter(jax.random.split(key, 48))
    f32 = jnp.float32

    def nrm(shape, s):
        return jax.random.normal(next(keys), shape, f32) * s

    n_pages = PAST_LEN // PAGE_SIZE
    n_pool = (5 * DEC_BATCH * n_pages) // 4
    win_buf = min(WINDOW, PAST_LEN)
    perm = jax.random.permutation(next(keys), n_pool)[:DEC_BATCH * n_pages]
    page_table = perm.reshape(DEC_BATCH, n_pages).astype(jnp.int32)
    dt_init = jnp.exp(jax.random.uniform(next(keys), (DEPTH, SSD_HEADS), f32, float(np.log(1e-3)), float(np.log(1e-1))))
    ssd_dt_bias = dt_init + jnp.log(-jnp.expm1(-dt_init))
    ssd_A_log = jnp.log(jax.random.uniform(next(keys), (DEPTH, SSD_HEADS), f32, 1.0, 16.0))
    return {
        'x_prompt': nrm((BATCH, SEQ, D_MODEL), 1.0),
        'x_sample': nrm((DEC_BATCH, DEC_SEQ, D_MODEL), 1.0),
        'cache_kv': nrm((DEPTH, n_pool, PAGE_SIZE, KV_SLOTS, NSA_HEAD_DIM), 1.0),
        'cache_win': nrm((DEPTH, DEC_BATCH, win_buf, 2, NSA_HEAD_DIM), 1.0),
        'state_sconv': nrm((DEPTH, DEC_BATCH, SC_WIDTH - 1, W_GROUP), 1.0),
        'state_ssm_conv': nrm((DEPTH, DEC_BATCH, SSD_CONV - 1, SSD_CONV_DIM), 1.0),
        'state_ssm': nrm((DEPTH, DEC_BATCH, SSD_HEADS, SSD_HEAD_DIM, SSD_STATE), 0.5),
        'state_cconv': nrm((DEPTH, DEC_BATCH, CF_WIDTH - 1, W_GROUP), 1.0),
        'page_table': page_table,
        'c_prompt': nrm((BATCH, D_MODEL), 1.0),
        'c_sample': nrm((DEC_BATCH, D_MODEL), 1.0),
        'norm_g': 1.0 + nrm((DEPTH, D_MODEL), 0.05),
        'w_ada': nrm((DEPTH, D_MODEL, 3 * D_MODEL), D_MODEL ** -0.5),
        'b_ada': nrm((DEPTH, 3 * D_MODEL), 0.02),
        'w_in': nrm((DEPTH, D_MODEL, D_IN), D_MODEL ** -0.5),
        'w_out': nrm((DEPTH, D_MIX, D_MODEL), D_MIX ** -0.5),
        'sc_w': nrm((DEPTH, SC_WIDTH, W_GROUP), SC_WIDTH ** -0.5),
        'ssd_conv_w': nrm((DEPTH, SSD_CONV, SSD_CONV_DIM), SSD_CONV ** -0.5),
        'ssd_conv_b': nrm((DEPTH, SSD_CONV_DIM), 0.02),
        'ssd_dt_bias': ssd_dt_bias,
        'ssd_A_log': ssd_A_log,
        'ssd_D': 1.0 + nrm((DEPTH, SSD_HEADS), 0.1),
        'ssd_norm_g': 1.0 + nrm((DEPTH, W_GROUP), 0.05),
        'cf_w': nrm((DEPTH, CF_WIDTH, W_GROUP), CF_WIDTH ** -0.5),
        'cf_b': nrm((DEPTH, W_GROUP), 0.02),
        'cf_ln_g': 1.0 + nrm((DEPTH, W_GROUP), 0.05),
        'cf_ln_b': nrm((DEPTH, W_GROUP), 0.02),
        'q_g': 1.0 + nrm((DEPTH, NSA_HEAD_DIM), 0.05),
        'k_g': 1.0 + nrm((DEPTH, 3, NSA_HEAD_DIM), 0.05),
        'cmp_pe': nrm((DEPTH, 2, CMP_BLOCK, NSA_HEAD_DIM), 0.1),
        'cmp_w1': nrm((DEPTH, 2, CMP_BLOCK, NSA_HEAD_DIM, CMP_HIDDEN), (CMP_BLOCK * NSA_HEAD_DIM) ** -0.5),
        'cmp_b1': nrm((DEPTH, 2, CMP_HIDDEN), 0.02),
        'cmp_w2': nrm((DEPTH, 2, CMP_HIDDEN, NSA_HEAD_DIM), CMP_HIDDEN ** -0.5),
        'cmp_b2': nrm((DEPTH, 2, NSA_HEAD_DIM), 0.02),
    }


def reference(x_prompt, x_sample, cache_kv, cache_win, state_sconv, state_ssm_conv, state_ssm, state_cconv,
              page_table, c_prompt, c_sample, norm_g, w_ada, b_ada, w_in, w_out, sc_w, ssd_conv_w, ssd_conv_b,
              ssd_dt_bias, ssd_A_log, ssd_D, ssd_norm_g, cf_w, cf_b, cf_ln_g, cf_ln_b, q_g, k_g,
              cmp_pe, cmp_w1, cmp_b1, cmp_w2, cmp_b2):
    nb = x_prompt.shape[0]
    dt = x_prompt.dtype
    z_sc = jnp.zeros((nb, SC_WIDTH - 1, W_GROUP), dt)
    z_bc = jnp.zeros((nb, SSD_CONV - 1, SSD_CONV_DIM), dt)
    z_h = jnp.zeros((nb, SSD_HEADS, SSD_HEAD_DIM, SSD_STATE), dt)
    z_cf = jnp.zeros((nb, CF_WIDTH - 1, W_GROUP), dt)
    yp, ys = x_prompt, x_sample
    acc_p, acc_s = [], []
    for l in range(DEPTH):
        lp = {'norm_g': norm_g[l], 'w_ada': w_ada[l], 'b_ada': b_ada[l], 'w_in': w_in[l], 'w_out': w_out[l],
              'sc_w': sc_w[l], 'ssd_conv_w': ssd_conv_w[l], 'ssd_conv_b': ssd_conv_b[l],
              'ssd_dt_bias': ssd_dt_bias[l], 'ssd_A_log': ssd_A_log[l], 'ssd_D': ssd_D[l],
              'ssd_norm_g': ssd_norm_g[l], 'cf_w': cf_w[l], 'cf_b': cf_b[l], 'cf_ln_g': cf_ln_g[l],
              'cf_ln_b': cf_ln_b[l], 'q_g': q_g[l], 'k_g': k_g[l], 'cmp_pe': cmp_pe[l], 'cmp_w1': cmp_w1[l],
              'cmp_b1': cmp_b1[l], 'cmp_w2': cmp_w2[l], 'cmp_b2': cmp_b2[l]}
        yp, st_p = trunk_layer(yp, c_prompt, lp, z_sc, z_bc, z_h, z_cf, attend_prompt)
        ys, st_s = trunk_layer(ys, c_sample, lp, state_sconv[l], state_ssm_conv[l], state_ssm[l], state_cconv[l],
                               make_attend_sample(cache_kv[l], cache_win[l], page_table))
        acc_p.append(st_p)
        acc_s.append(st_s)
    kv_p, win_p, sc_p, bc_p, h_p, cf_p = [jnp.stack(a) for a in zip(*acc_p)]
    kv_s, win_s, sc_s, bc_s, h_s, cf_s = [jnp.stack(a) for a in zip(*acc_s)]
    return (yp, ys, kv_p, kv_s, win_p, win_s, sc_p, sc_s, bc_p, bc_s, h_p, h_s, cf_p, cf_s)
```

```python
import functools

import numpy as np
import jax
import jax.numpy as jnp
from jax import lax
from jax.experimental import pallas as pl
from jax.experimental.pallas import tpu as pltpu

F32 = jnp.float32
BF16 = jnp.bfloat16
I32 = jnp.int32

D_MODEL = 1024
WG = 256
SC_K = 3
SSD_K = 4
SSD_CD = 768
SSD_H = 4
SSD_P = 64
SSD_S = 128
CF_K = 31
HD = 64
NH = 4
CMP_STRIDE = 16
SEL_BLOCK = 64
SEL_SHIFT = 6
SEL_TOPN = 16
WINDOW = 512
PAGE = 128
EPS = 1e-6
NEG = -1e30
HALO = 32
LANES = 128

N_ABC = 2944
N_D = 1024
N_IN = N_ABC + N_D
O_BZ, O_BX, O_BDT, O_CGLU, O_CG = 1024, 1280, 2048, 2176, 2688
O_DQ, O_DG, O_DKV, O_DBG = 0, 256, 512, 896

VMEM_BIG = 56 * 1024 * 1024


def _silu(x):
    return x * jax.nn.sigmoid(x)


def _softplus(x):
    return jnp.maximum(x, 0.0) + jnp.log1p(jnp.exp(-jnp.abs(x)))


def _bdot(a, b):
    return jnp.dot(a.astype(BF16), b.astype(BF16), preferred_element_type=F32)


def _bdot_nt(a, b):
    return lax.dot_general(a.astype(BF16), b.astype(BF16), (((1,), (1,)), ((), ())),
                           preferred_element_type=F32)


def _split3(a):
    a1 = a.astype(BF16)
    r1 = a - a1.astype(F32)
    a2 = r1.astype(BF16)
    a3 = (r1 - a2.astype(F32)).astype(BF16)
    return a1, a2, a3


def _dot_exact_rhs(a, m_bf16):
    a1, a2, a3 = _split3(a)
    d = functools.partial(jnp.dot, preferred_element_type=F32)
    return d(a1, m_bf16) + d(a2, m_bf16) + d(a3, m_bf16)


def _dot_exact_lhs(m_bf16, a):
    a1, a2, a3 = _split3(a)
    d = functools.partial(jnp.dot, preferred_element_type=F32)
    return d(m_bf16, a1) + d(m_bf16, a2) + d(m_bf16, a3)


def _group_ones(n, group):
    sh = group.bit_length() - 1
    assert group == 1 << sh
    r = lax.broadcasted_iota(I32, (n, n), 0) >> sh
    c = lax.broadcasted_iota(I32, (n, n), 1) >> sh
    return jnp.where(r == c, 1.0, 0.0).astype(BF16)


def _group_rms(x, group):
    ssq = _dot_exact_rhs(x * x, _group_ones(x.shape[-1], group))
    return x * lax.rsqrt(ssq * (1.0 / group) + EPS)


def _rms(x):
    return x * lax.rsqrt(jnp.mean(x * x, axis=-1, keepdims=True) + EPS)


def _modulated_norm(x, g, scale, shift):
    return _rms(x) * g * (1.0 + scale) + shift


def _call(kernel, **kw):
    return pl.pallas_call(kernel, **kw)


def _cparams(sem=None, vmem=None):
    kw = {}
    if sem is not None:
        kw["dimension_semantics"] = sem
    if vmem is not None:
        kw["vmem_limit_bytes"] = vmem
    return pltpu.CompilerParams(**kw)


def _full(shape):
    nd = len(shape)
    return pl.BlockSpec(shape, lambda *_: (0,) * nd)


def _ada_kernel(c_ref, w_ref, b_ref, o_ref):
    o_ref[0] = _bdot(_silu(c_ref[...]), w_ref[0]) + b_ref[0]


def _ada(c_all, w_ada, b_ada):
    depth = w_ada.shape[0]
    rows = c_all.shape[0]
    tn = 512
    return _call(
        _ada_kernel,
        grid=(depth, 3 * D_MODEL // tn),
        in_specs=[pl.BlockSpec((rows, D_MODEL), lambda l, j: (0, 0)),
                  pl.BlockSpec((1, D_MODEL, tn), lambda l, j: (l, 0, j)),
                  pl.BlockSpec((1, 1, tn), lambda l, j: (l, 0, j))],
        out_specs=pl.BlockSpec((1, rows, tn), lambda l, j: (l, 0, j)),
        out_shape=jax.ShapeDtypeStruct((depth, rows, 3 * D_MODEL), F32),
        name="ada",
    )(c_all, w_ada, b_ada.reshape(depth, 1, 3 * D_MODEL))


def _pe_kernel(pe_ref, w_ref, o_ref):
    o_ref[0] = jnp.sum(pe_ref[0] * w_ref[0], axis=0, keepdims=True)


def _pe_terms(cmp_pe, cmp_w1):
    depth = cmp_pe.shape[0]
    n = depth * 2
    kk = cmp_pe.shape[2] * cmp_pe.shape[3]
    hid = cmp_w1.shape[-1]
    return _call(
        _pe_kernel,
        grid=(n,),
        in_specs=[pl.BlockSpec((1, kk, 1), lambda i: (i, 0, 0)),
                  pl.BlockSpec((1, kk, hid), lambda i: (i, 0, 0))],
        out_specs=pl.BlockSpec((1, 1, hid), lambda i: (i, 0, 0)),
        out_shape=jax.ShapeDtypeStruct((n, 1, hid), F32),
        name="pe_term",
    )(cmp_pe.reshape(n, kk, 1), cmp_w1.reshape(n, kk, hid))


def _inproj_kernel(x_ref, ada_ref, g_ref, w_ref, oabc_ref, od_ref):
    shift = ada_ref[:, 0:D_MODEL]
    scale = ada_ref[:, D_MODEL:2 * D_MODEL]
    h = _modulated_norm(x_ref[...], g_ref[...], scale, shift)
    p = _bdot(h, w_ref[...])
    oabc_ref[...] = p[:, :N_ABC]
    od_ref[...] = p[:, N_ABC:]


def _inproj(x, ada_row, g, w):
    s = x.shape[0]
    tr = 256
    return _call(
        _inproj_kernel,
        grid=(s // tr,),
        in_specs=[pl.BlockSpec((tr, D_MODEL), lambda i: (i, 0)),
                  _full((1, 3 * D_MODEL)), _full((1, D_MODEL)), _full((D_MODEL, N_IN))],
        out_specs=[pl.BlockSpec((tr, N_ABC), lambda i: (i, 0)),
                   pl.BlockSpec((tr, N_D), lambda i: (i, 0))],
        out_shape=[jax.ShapeDtypeStruct((s, N_ABC), F32), jax.ShapeDtypeStruct((s, N_D), F32)],
        compiler_params=_cparams(("arbitrary",), VMEM_BIG),
        name="inproj",
    )(x, ada_row, g, w)


def _tile_conv(ext_ref, u, w_ref, taps, t):
    ext_ref[HALO:HALO + t, :] = u
    acc = None
    for k in range(taps):
        o = HALO - (taps - 1) + k
        term = w_ref[k:k + 1, :] * ext_ref[o:o + t, :]
        acc = term if acc is None else acc + term
    return acc


def _ssd_chunk(xs, bm, cm, dtp, a, hst_ref):
    t = xs.shape[0]
    ri = lax.broadcasted_iota(I32, (t, t), 0)
    ci = lax.broadcasted_iota(I32, (t, t), 1)
    tril = ri >= ci
    trilb = jnp.where(tril, 1.0, 0.0).astype(BF16)
    triub = jnp.where(ri <= ci, 1.0, 0.0).astype(BF16)
    acs_col = _dot_exact_lhs(trilb, a)
    acs_row = _dot_exact_rhs(a.T, triub)
    ys = []
    for g in range(2):
        bg = bm[:, SSD_S * g:SSD_S * (g + 1)]
        cg = cm[:, SSD_S * g:SSD_S * (g + 1)]
        gram = _bdot_nt(cg, bg)
        xte = []
        for hh in range(2):
            h = 2 * g + hh
            ac = acs_col[:, h:h + 1]
            ar = acs_row[h:h + 1, :]
            a_last = acs_col[t - 1:t, h:h + 1]
            decay = jnp.where(tril, jnp.exp(jnp.where(tril, ac - ar, 0.0)), 0.0)
            xdt = xs[:, SSD_P * h:SSD_P * (h + 1)] * dtp[:, h:h + 1]
            y_diag = _bdot(gram * decay, xdt)
            hprev = hst_ref[SSD_P * h:SSD_P * (h + 1), :]
            y_off = _bdot_nt(cg, hprev) * jnp.exp(ac)
            ys.append(y_diag + y_off)
            xte.append(xdt * jnp.exp(a_last - ac))
        x2t = jnp.concatenate(xte, axis=1).T
        states = _bdot(x2t, bg)
        for hh in range(2):
            h = 2 * g + hh
            cd = jnp.exp(acs_col[t - 1:t, h:h + 1])
            sl = slice(SSD_P * h, SSD_P * (h + 1))
            hst_ref[sl, :] = cd * hst_ref[sl, :] + states[SSD_P * hh:SSD_P * (hh + 1), :]
    return jnp.concatenate(ys, axis=1)


def _mix_kernel(p_ref, scw_ref, bcw_ref, bcb_ref, dtb_ref, alog_ref, dvec_ref, ng_ref,
                cfw_ref, cfb_ref, lng_ref, lnb_ref,
                y_ref, sc_ref, bc_ref, hs_ref, cf_ref,
                exta, extb, extc, hst):
    t = p_ref.shape[0]
    i = pl.program_id(0)

    @pl.when(i == 0)
    def _():
        exta[0:HALO, :] = jnp.zeros((HALO, WG), F32)
        extb[0:HALO, :] = jnp.zeros((HALO, SSD_CD), F32)
        extc[0:HALO, :] = jnp.zeros((HALO, WG), F32)
        hst[...] = jnp.zeros_like(hst)

    u_a = p_ref[:, 512:768] * p_ref[:, 0:256]
    conv_a = _tile_conv(exta, u_a, scw_ref, SC_K, t)
    y_a = _silu(p_ref[:, 768:1024]) * p_ref[:, 256:512] * conv_a

    conv_b = _tile_conv(extb, p_ref[:, O_BX:O_BX + SSD_CD], bcw_ref, SSD_K, t) + bcb_ref[...]
    xbc = _silu(conv_b)
    xs = xbc[:, 0:WG]
    dtp = _softplus(p_ref[:, O_BDT:O_BDT + LANES] + dtb_ref[...])
    a = dtp * (-jnp.exp(alog_ref[...]))
    y_ssd = _ssd_chunk(xs, xbc[:, WG:WG + 2 * SSD_S], xbc[:, WG + 2 * SSD_S:], dtp, a, hst)
    y_b = y_ssd + dvec_ref[...] * xs
    y_b = _rms(y_b * _silu(p_ref[:, O_BZ:O_BZ + WG])) * ng_ref[...]

    glu = p_ref[:, O_CGLU:O_CGLU + WG] * jax.nn.sigmoid(p_ref[:, O_CGLU + WG:O_CGLU + 2 * WG])
    u_c = _tile_conv(extc, glu, cfw_ref, CF_K, t) + cfb_ref[...]
    mu = jnp.mean(u_c, axis=-1, keepdims=True)
    var = jnp.mean(jnp.square(u_c - mu), axis=-1, keepdims=True)
    ln = (u_c - mu) * lax.rsqrt(var + EPS) * lng_ref[...] + lnb_ref[...]
    y_c = _silu(p_ref[:, O_CG:O_CG + WG]) * _silu(ln)

    y_ref[:, 0:WG] = y_a
    y_ref[:, WG:2 * WG] = y_b
    y_ref[:, 2 * WG:3 * WG] = y_c

    sc_ref[...] = exta[HALO + t - (SC_K - 1):HALO + t, :]
    bc_ref[...] = extb[HALO + t - (SSD_K - 1):HALO + t, :]
    cf_ref[...] = extc[HALO + t - (CF_K - 1):HALO + t, :]
    hs_ref[...] = hst[...]
    exta[0:HALO, :] = exta[t:t + HALO, :]
    extb[0:HALO, :] = extb[t:t + HALO, :]
    extc[0:HALO, :] = extc[t:t + HALO, :]


def _mix(pabc, wts):
    s = pabc.shape[0]
    t = SSD_S
    small = [wts[k] for k in ("sc_w", "bc_w", "bc_b", "dtb128", "alog128", "dvec", "ssd_ng",
                              "cf_w", "cf_b", "ln_g", "ln_b")]
    return _call(
        _mix_kernel,
        grid=(s // t,),
        in_specs=[pl.BlockSpec((t, N_ABC), lambda i: (i, 0))] + [_full(w.shape) for w in small],
        out_specs=[pl.BlockSpec((t, 3 * WG), lambda i: (i, 0)),
                   _full((SC_K - 1, WG)), _full((SSD_K - 1, SSD_CD)),
                   _full((SSD_H * SSD_P, SSD_S)), _full((CF_K - 1, WG))],
        out_shape=[jax.ShapeDtypeStruct((s, 3 * WG), F32),
                   jax.ShapeDtypeStruct((SC_K - 1, WG), F32),
                   jax.ShapeDtypeStruct((SSD_K - 1, SSD_CD), F32),
                   jax.ShapeDtypeStruct((SSD_H * SSD_P, SSD_S), F32),
                   jax.ShapeDtypeStruct((CF_K - 1, WG), F32)],
        scratch_shapes=[pltpu.VMEM((HALO + t, WG), F32), pltpu.VMEM((HALO + t, SSD_CD), F32),
                        pltpu.VMEM((HALO + t, WG), F32), pltpu.VMEM((SSD_H * SSD_P, SSD_S), F32)],
        compiler_params=_cparams(("arbitrary",)),
        name="mix",
    )(pabc, *small)


def _nsa_rows(pd, qg, kg1, kg2):
    q = _group_rms(pd[:, O_DQ:O_DQ + WG], HD) * qg * (HD ** -0.5)
    kv = pd[:, O_DKV:O_DKV + 6 * HD]
    k_slc = _rms(kv[:, 2 * HD:3 * HD]) * kg1
    k_win = _rms(kv[:, 4 * HD:5 * HD]) * kg2
    rows = jnp.concatenate([kv[:, 0:2 * HD], k_slc, kv[:, 3 * HD:4 * HD]], axis=1)
    win = jnp.concatenate([k_win, kv[:, 5 * HD:6 * HD]], axis=1)
    gates = jax.nn.sigmoid(pd[:, O_DBG:O_DBG + LANES])
    return q, rows, win, gates


def _nsaprep_kernel(pd_ref, qg_ref, kg_ref, qT_ref, kb_ref, vT_ref, kvp_ref, winT_ref, gT_ref):
    tr = pd_ref.shape[0]
    q, rows, win, gates = _nsa_rows(pd_ref[...], qg_ref[...], kg_ref[1:2, :], kg_ref[2:3, :])
    qT_ref[...] = q.T.astype(BF16)
    kb_ref[...] = jnp.concatenate([rows[:, 2 * HD:3 * HD], win[:, 0:HD]], axis=1).astype(BF16)
    rows_t = rows.T
    win_t = win.T
    vT_ref[...] = jnp.concatenate([rows_t[3 * HD:4 * HD, :], win_t[HD:2 * HD, :]], axis=0).astype(BF16)
    for p in range(tr // PAGE):
        kvp_ref[p] = rows_t[:, PAGE * p:PAGE * (p + 1)].reshape(4, HD, PAGE)
    winT_ref[...] = win_t.reshape(2, HD, tr)
    gT_ref[...] = gates.T[0:16, :]


def _nsaprep(pd, qg, kg):
    s = pd.shape[0]
    tr = 256
    return _call(
        _nsaprep_kernel,
        grid=(s // tr,),
        in_specs=[pl.BlockSpec((tr, N_D), lambda i: (i, 0)), _full(qg.shape), _full(kg.shape)],
        out_specs=[pl.BlockSpec((WG, tr), lambda i: (0, i)),
                   pl.BlockSpec((tr, 2 * HD), lambda i: (i, 0)),
                   pl.BlockSpec((2 * HD, tr), lambda i: (0, i)),
                   pl.BlockSpec((tr // PAGE, 4, HD, PAGE), lambda i: (i, 0, 0, 0)),
                   pl.BlockSpec((2, HD, tr), lambda i: (0, 0, i)),
                   pl.BlockSpec((16, tr), lambda i: (0, i))],
        out_shape=[jax.ShapeDtypeStruct((WG, s), BF16),
                   jax.ShapeDtypeStruct((s, 2 * HD), BF16),
                   jax.ShapeDtypeStruct((2 * HD, s), BF16),
                   jax.ShapeDtypeStruct((s // PAGE, 4, HD, PAGE), F32),
                   jax.ShapeDtypeStruct((2, HD, s), F32),
                   jax.ShapeDtypeStruct((16, s), F32)],
        compiler_params=_cparams(("arbitrary",)),
        name="nsaprep",
    )(pd, qg, kg)


def _cmp_tail(pre_ref, nc, pe_ref, b1_ref, w2_ref, b2_ref, kg0):
    hid = b1_ref.shape[-1]
    outs = []
    for s in range(2):
        first = pre_ref[0:nc, 2 * s * hid:(2 * s + 1) * hid]
        second = pre_ref[pl.ds(1, nc), (2 * s + 1) * hid:(2 * s + 2) * hid]
        h = _silu(first + second + pe_ref[s] + b1_ref[s])
        outs.append(_bdot(h, w2_ref[s]) + b2_ref[s])
    return _rms(outs[0]) * kg0, outs[1]


def _pcompress_kernel(kv_ref, wbd_ref, pe_ref, b1_ref, w2_ref, b2_ref, kg_ref, kc_ref, vcT_ref, pre):
    nc = kv_ref.shape[0] // CMP_STRIDE
    acc = None
    for j in range(CMP_STRIDE):
        term = _bdot(kv_ref[pl.ds(j, nc, stride=CMP_STRIDE), :], wbd_ref[j])
        acc = term if acc is None else acc + term
    pre[0:nc, :] = acc
    pre[nc:nc + 8, :] = jnp.zeros((8, pre.shape[1]), F32)
    kc, vc = _cmp_tail(pre, nc, pe_ref, b1_ref, w2_ref, b2_ref, kg_ref[0:1, :])
    kc_ref[...] = kc.astype(BF16)
    vcT_ref[...] = jnp.concatenate([vc, jnp.zeros_like(vc)], axis=1).T[0:HD, :].astype(BF16)


def _pcompress(pd, wts):
    s = pd.shape[0]
    nc = s // CMP_STRIDE
    ws = [wts[k] for k in ("wbd", "pe", "cmp_b1", "cmp_w2", "cmp_b2", "k_g")]
    return _call(
        _pcompress_kernel,
        grid=(1,),
        in_specs=[pl.BlockSpec((s, 2 * HD), lambda i: (0, O_DKV // (2 * HD)))] + [_full(w.shape) for w in ws],
        out_specs=[_full((nc, HD)), _full((HD, nc))],
        out_shape=[jax.ShapeDtypeStruct((nc, HD), BF16), jax.ShapeDtypeStruct((HD, nc), BF16)],
        scratch_shapes=[pltpu.VMEM((nc + 8, 4 * wts["cmp_b1"].shape[-1]), F32)],
        compiler_params=_cparams(("arbitrary",), VMEM_BIG),
        name="pcompress",
    )(pd, *ws)


def _masked_softmax_cols(s, mask):
    sm = jnp.where(mask, s, NEG)
    m = jnp.max(sm, axis=0, keepdims=True)
    e = jnp.where(mask, jnp.exp(sm - m), 0.0)
    l = jnp.sum(e, axis=0, keepdims=True)
    return e * jnp.where(l > 0.0, 1.0 / l, 0.0)


def _flash_cols(carry, s, vt):
    m, l, acc = carry
    mn = jnp.maximum(m, jnp.max(s, axis=0, keepdims=True))
    alpha = jnp.exp(m - mn)
    p = jnp.exp(s - mn)
    l = alpha * l + jnp.sum(p, axis=0, keepdims=True)
    acc = alpha * acc + jnp.dot(vt, p.astype(BF16), preferred_element_type=F32)
    return mn, l, acc


def _attn_kernel(qT_ref, kb_ref, vT_ref, kc_ref, vcT_ref, gT_ref, dg_ref, o_ref, pg_scr, bias_scr,
                 *, n_sel, tk):
    qb = qT_ref.shape[1]
    nq = NH * qb
    i = pl.program_id(0)
    q0 = i * qb
    qt = qT_ref[...]
    qall = jnp.concatenate([qt[HD * h:HD * (h + 1), :] for h in range(NH)], axis=1)
    zq = jnp.zeros_like(qall)
    q_slc = jnp.concatenate([qall, zq], axis=0)
    q_win = jnp.concatenate([zq, qall], axis=0)

    nc = kc_ref.shape[0]
    s_c = jnp.dot(kc_ref[...], qall, preferred_element_type=F32)
    c_io = lax.broadcasted_iota(I32, (nc, nq), 0)
    qpos = q0 + (lax.broadcasted_iota(I32, (nc, nq), 1) & (qb - 1))
    p_c = _masked_softmax_cols(s_c, c_io * CMP_STRIDE + (2 * CMP_STRIDE - 1) <= qpos)
    o_c = jnp.dot(vcT_ref[...], p_c.astype(BF16), preferred_element_type=F32)

    pg = p_c[:, 0:qb]
    for h in range(1, NH):
        pg = pg + p_c[:, qb * h:qb * (h + 1)]
    pg_scr[0:8, :] = jnp.zeros((8, qb), F32)
    pg_scr[8:8 + nc, :] = pg
    pg_scr[8 + nc:16 + nc, :] = jnp.zeros((8, qb), F32)
    ns = nc // 4

    def strided(k):
        return pg_scr[pl.ds(8 + k, ns, stride=4), :]

    p_sel = 0.5 * strided(-1) + strided(0)
    p_sel = p_sel + strided(1)
    p_sel = p_sel + strided(2)
    p_sel = p_sel + 0.5 * strided(3)

    j_io = lax.broadcasted_iota(I32, (ns, qb), 0)
    j_f = j_io.astype(F32)
    cur = (q0 + lax.broadcasted_iota(I32, (ns, qb), 1)) >> SEL_SHIFT
    validb = j_io <= cur
    forced = validb & ((j_io == 0) | (j_io > cur - 2))
    score = jnp.where(forced, jnp.inf, jnp.where(validb, p_sel, -jnp.inf))
    selm = jnp.zeros((ns, qb), F32)
    for _ in range(n_sel):
        free = selm == 0.0
        mx = jnp.max(jnp.where(free, score, -jnp.inf), axis=0, keepdims=True)
        hit = free & (score == mx)
        first = jnp.min(jnp.where(hit, j_f, float(ns)), axis=0, keepdims=True)
        selm = jnp.where(j_f == first, 1.0, selm)
    bias_scr[...] = jnp.where((selm > 0.0) & validb, 0.0, NEG)

    bpt = tk // SEL_BLOCK

    def sel_scores(t):
        kt = kb_ref[pl.ds(pl.multiple_of(t * tk, tk), tk), :]
        s = jnp.dot(kt, q_slc, preferred_element_type=F32)
        rows = [jnp.broadcast_to(bias_scr[pl.ds(t * bpt + b, 1), :], (SEL_BLOCK, qb)) for b in range(bpt)]
        bias = jnp.concatenate(rows, axis=0)
        return s + jnp.concatenate([bias] * NH, axis=1)

    def v_slc(t):
        return vT_ref[0:HD, pl.ds(pl.multiple_of(t * tk, tk), tk)]

    def sel_body(t, carry):
        return _flash_cols(carry, sel_scores(t), v_slc(t))

    init = (jnp.full((1, nq), 2 * NEG, F32), jnp.zeros((1, nq), F32), jnp.zeros((HD, nq), F32))
    t_last = q0 // tk
    carry = lax.fori_loop(0, t_last, sel_body, init)
    kpos = t_last * tk + lax.broadcasted_iota(I32, (tk, nq), 0)
    qpos_t = q0 + (lax.broadcasted_iota(I32, (tk, nq), 1) & (qb - 1))
    s_last = jnp.where(kpos <= qpos_t, sel_scores(t_last), NEG)
    _, l_s, acc_s = _flash_cols(carry, s_last, v_slc(t_last))
    o_s = acc_s / l_s

    wk = WINDOW + qb
    start = pl.multiple_of(jnp.maximum(q0 - WINDOW, 0), qb)
    s_w = jnp.dot(kb_ref[pl.ds(start, wk), :], q_win, preferred_element_type=F32)
    dpos = (q0 + (lax.broadcasted_iota(I32, (wk, nq), 1) & (qb - 1))
            - (start + lax.broadcasted_iota(I32, (wk, nq), 0)))
    p_w = _masked_softmax_cols(s_w, (dpos >= 0) & (dpos <= WINDOW))
    o_w = jnp.dot(vT_ref[HD:2 * HD, pl.ds(start, wk)], p_w.astype(BF16), preferred_element_type=F32)

    heads = []
    for h in range(NH):
        sl = slice(qb * h, qb * (h + 1))
        heads.append(gT_ref[3 * h:3 * h + 1, :] * o_c[:, sl] + gT_ref[3 * h + 1:3 * h + 2, :] * o_s[:, sl]
                     + gT_ref[3 * h + 2:3 * h + 3, :] * o_w[:, sl])
    o_rows = jnp.concatenate(heads, axis=0).T
    o_ref[...] = _silu(dg_ref[...]) * o_rows


def _attn(qT, kb, vT, kc, vcT, gT, pd):
    s = kb.shape[0]
    qb = LANES
    nc = kc.shape[0]
    tk = min(512, s)
    kern = functools.partial(_attn_kernel, n_sel=min(SEL_TOPN, s // SEL_BLOCK), tk=tk)
    return _call(
        kern,
        grid=(s // qb,),
        in_specs=[pl.BlockSpec((WG, qb), lambda i: (0, i)),
                  _full(kb.shape), _full(vT.shape), _full(kc.shape), _full(vcT.shape),
                  pl.BlockSpec((16, qb), lambda i: (0, i)),
                  pl.BlockSpec((qb, WG), lambda i: (i, O_DG // WG))],
        out_specs=pl.BlockSpec((qb, WG), lambda i: (i, 0)),
        out_shape=jax.ShapeDtypeStruct((s, WG), F32),
        scratch_shapes=[pltpu.VMEM((nc + 16, qb), F32), pltpu.VMEM((nc // 4, qb), F32)],
        compiler_params=_cparams(("arbitrary",), VMEM_BIG),
        name="attn",
    )(qT, kb, vT, kc, vcT, gT, pd)


def _outproj_kernel(x_ref, yabc_ref, yd_ref, ada_ref, w_ref, o_ref):
    y = jnp.concatenate([yabc_ref[...], yd_ref[...]], axis=1)
    o_ref[...] = x_ref[...] + ada_ref[:, 2 * D_MODEL:3 * D_MODEL] * _bdot(y, w_ref[...])


def _outproj(x, yabc, yd, ada_rows, w):
    s = x.shape[0]
    tr = min(512, s)
    ar = ada_rows.shape[0]
    ada_spec = (_full((1, 3 * D_MODEL)) if ar == 1 else pl.BlockSpec((tr, 3 * D_MODEL), lambda i: (i, 0)))
    return _call(
        _outproj_kernel,
        grid=(s // tr,),
        in_specs=[pl.BlockSpec((tr, D_MODEL), lambda i: (i, 0)),
                  pl.BlockSpec((tr, 3 * WG), lambda i: (i, 0)),
                  pl.BlockSpec((tr, WG), lambda i: (i, 0)),
                  ada_spec, _full((4 * WG, D_MODEL))],
        out_specs=pl.BlockSpec((tr, D_MODEL), lambda i: (i, 0)),
        out_shape=jax.ShapeDtypeStruct((s, D_MODEL), F32),
        compiler_params=_cparams(("arbitrary",)),
        name="outproj",
    )(x, yabc, yd, ada_rows, w)


def _pad_t(scr, x):
    nb = x.shape[0]
    scr[...] = jnp.zeros_like(scr)
    scr[0:nb, :] = x
    return scr[...].T


def _spre_kernel(x_ref, ada_ref, g_ref, w_ref, sc_ref, bc_ref, hs_ref, cf_ref,
                 scw_ref, bcw_ref, bcb_ref, dtb_ref, alog_ref, dvec_ref, ng_ref,
                 cfw_ref, cfb_ref, lng_ref, lnb_ref, qg_ref, kg_ref,
                 y_ref, q_ref, rows_ref, win_ref, gates_ref, sgd_ref,
                 sco_ref, bco_ref, hso_ref, cfo_ref,
                 xbc_scr, padA, padB):
    nb = x_ref.shape[0]
    h = _modulated_norm(x_ref[...], g_ref[...], ada_ref[:, D_MODEL:2 * D_MODEL], ada_ref[:, 0:D_MODEL])
    p = _bdot(h, w_ref[...])

    u_a = p[:, 512:768] * p[:, 0:256]
    conv_a = scw_ref[0:1, :] * sc_ref[0] + scw_ref[1:2, :] * sc_ref[1] + scw_ref[2:3, :] * u_a
    y_ref[:, 0:WG] = _silu(p[:, 768:1024]) * p[:, 256:512] * conv_a
    sco_ref[0] = sc_ref[1]
    sco_ref[1] = u_a

    bx = p[:, O_BX:O_BX + SSD_CD]
    conv_b = (bcw_ref[0:1, :] * bc_ref[0] + bcw_ref[1:2, :] * bc_ref[1] + bcw_ref[2:3, :] * bc_ref[2]
              + bcw_ref[3:4, :] * bx + bcb_ref[...])
    bco_ref[0] = bc_ref[1]
    bco_ref[1] = bc_ref[2]
    bco_ref[2] = bx
    xbc = _silu(conv_b)
    xbc_scr[...] = xbc
    xs = xbc[:, 0:WG]
    dtp = _softplus(p[:, O_BDT:O_BDT + LANES] + dtb_ref[...])
    dt_c = jnp.concatenate([jnp.broadcast_to(dtp[:, hh:hh + 1], (nb, SSD_P)) for hh in range(SSD_H)], axis=1)
    dec_t = _pad_t(padA, jnp.exp(dt_c * (-jnp.exp(alog_ref[...]))))
    xdt_t = _pad_t(padB, xs * dt_c)
    lane = lax.broadcasted_iota(I32, (SSD_H * SSD_P, LANES), 1)

    def sample_step(n, ymat):
        oh = lane == n
        dcol = jnp.sum(jnp.where(oh, dec_t, 0.0), axis=1, keepdims=True)
        xcol = jnp.sum(jnp.where(oh, xdt_t, 0.0), axis=1, keepdims=True)
        brow = xbc_scr[pl.ds(n, 1), WG:WG + 2 * SSD_S]
        crow = xbc_scr[pl.ds(n, 1), WG + 2 * SSD_S:WG + 4 * SSD_S]
        half = 2 * SSD_P
        bfull = jnp.concatenate([jnp.broadcast_to(brow[:, 0:SSD_S], (half, SSD_S)),
                                 jnp.broadcast_to(brow[:, SSD_S:], (half, SSD_S))], axis=0)
        cfull = jnp.concatenate([jnp.broadcast_to(crow[:, 0:SSD_S], (half, SSD_S)),
                                 jnp.broadcast_to(crow[:, SSD_S:], (half, SSD_S))], axis=0)
        hn = dcol * hs_ref[n] + xcol * bfull
        hso_ref[n] = hn
        ycol = jnp.sum(hn * cfull, axis=1, keepdims=True)
        return jnp.where(oh, ycol, ymat)

    ymat = lax.fori_loop(0, nb, sample_step, jnp.zeros((SSD_H * SSD_P, LANES), F32))
    y_b = ymat.T[0:nb, :] + dvec_ref[...] * xs
    y_ref[:, WG:2 * WG] = _rms(y_b * _silu(p[:, O_BZ:O_BZ + WG])) * ng_ref[...]

    glu = p[:, O_CGLU:O_CGLU + WG] * jax.nn.sigmoid(p[:, O_CGLU + WG:O_CGLU + 2 * WG])
    u_c = cfw_ref[CF_K - 1:CF_K, :] * glu + cfb_ref[...]
    for k in range(CF_K - 1):
        u_c = u_c + cfw_ref[k:k + 1, :] * cf_ref[k]
    for k in range(CF_K - 2):
        cfo_ref[k] = cf_ref[k + 1]
    cfo_ref[CF_K - 2] = glu
    mu = jnp.mean(u_c, axis=-1, keepdims=True)
    var = jnp.mean(jnp.square(u_c - mu), axis=-1, keepdims=True)
    ln = (u_c - mu) * lax.rsqrt(var + EPS) * lng_ref[...] + lnb_ref[...]
    y_ref[:, 2 * WG:3 * WG] = _silu(p[:, O_CG:O_CG + WG]) * _silu(ln)

    pd = p[:, N_ABC:]
    q, rows, win, gates = _nsa_rows(pd, qg_ref[...], kg_ref[1:2, :], kg_ref[2:3, :])
    q_ref[...] = q
    rows_ref[...] = rows
    win_ref[...] = win
    gates_ref[...] = gates
    sgd_ref[...] = _silu(pd[:, O_DG:O_DG + WG])


def _spre(x, ada_rows, g, w, st, wts):
    nb = x.shape[0]
    small = [wts[k] for k in ("sc_w", "bc_w", "bc_b", "dtb128", "alog256", "dvec", "ssd_ng",
                              "cf_w", "cf_b", "ln_g", "ln_b", "q_g", "k_g")]
    ins = [x, ada_rows, g, w, st["sc"], st["bc"], st["hs"], st["cf"]] + small
    outs = [jax.ShapeDtypeStruct((nb, 3 * WG), F32),
            jax.ShapeDtypeStruct((nb, WG), F32),
            jax.ShapeDtypeStruct((nb, WG), F32),
            jax.ShapeDtypeStruct((nb, 2 * HD), F32),
            jax.ShapeDtypeStruct((nb, LANES), F32),
            jax.ShapeDtypeStruct((nb, WG), F32),
            jax.ShapeDtypeStruct(st["sc"].shape, F32),
            jax.ShapeDtypeStruct(st["bc"].shape, F32),
            jax.ShapeDtypeStruct(st["hs"].shape, F32),
            jax.ShapeDtypeStruct(st["cf"].shape, F32)]
    return _call(
        _spre_kernel,
        grid=(1,),
        in_specs=[_full(a.shape) for a in ins],
        out_specs=[_full(o.shape) for o in outs],
        out_shape=outs,
        scratch_shapes=[pltpu.VMEM((nb, SSD_CD), F32), pltpu.VMEM((LANES, WG), F32),
                        pltpu.VMEM((LANES, WG), F32)],
        compiler_params=_cparams(("arbitrary",), VMEM_BIG),
        name="sample_pre",
    )(*ins)


def _scmp_kernel(pt_ref, *refs, pg, n_groups, layer):
    del pt_ref, layer
    pages = refs[:pg]
    (perm_ref, wbd_ref, pe_ref, b1_ref, w2_ref, b2_ref, kg_ref, q_ref, imp_ref,
     oc_ref, psel_ref, permscr, pre) = refs[pg:]
    g = pl.program_id(1)
    cpp = PAGE // CMP_STRIDE
    for p in range(pg):
        both = pages[p][0, 0].reshape(2 * HD, PAGE)
        kp = _bdot_nt(perm_ref[...], both)
        for j in range(CMP_STRIDE):
            permscr[j, cpp * p:cpp * (p + 1), :] = kp[cpp * j:cpp * (j + 1), :]
    acc = None
    for j in range(CMP_STRIDE):
        term = _bdot(permscr[j], wbd_ref[j])
        acc = term if acc is None else acc + term
    rows = cpp * pg
    pre[pl.ds(pl.multiple_of(g * rows, rows), rows), :] = acc

    @pl.when(g == n_groups - 1)
    def _():
        nc = rows * n_groups
        pre[nc:nc + 8, :] = jnp.zeros((8, pre.shape[1]), F32)
        kc, vc = _cmp_tail(pre, nc, pe_ref, b1_ref, w2_ref, b2_ref, kg_ref[0:1, :])
        s = _bdot_nt(q_ref[0], kc)
        c_io = lax.broadcasted_iota(I32, (8, nc), 1)
        valid = c_io <= nc - 2
        sm = jnp.where(valid, s, NEG)
        m = jnp.max(sm, axis=1, keepdims=True)
        e = jnp.where(valid, jnp.exp(sm - m), 0.0)
        p = e / jnp.sum(e, axis=1, keepdims=True)
        oc_ref[0] = _bdot(p, vc)
        ps8 = _dot_exact_rhs(p, imp_ref[...])
        psel_ref[0] = ps8[0:1, :] + ps8[1:2, :] + ps8[2:3, :] + ps8[3:4, :]


def _scmp(ckv_t, page_table, layer, q8, wts, consts):
    nb, n_pages = page_table.shape
    pg = min(16, n_pages)
    n_groups = n_pages // pg
    nc = n_pages * (PAGE // CMP_STRIDE)
    hid4 = 4 * wts["cmp_b1"].shape[-1]
    nsp = consts["imp"].shape[1]

    def page_spec(k):
        return pl.BlockSpec((1, 1, 2, HD, PAGE), lambda n, g, pt: (layer, pt[n, g * pg + k], 0, 0, 0))

    ws = [consts["perm"], wts["wbd"], wts["pe"], wts["cmp_b1"], wts["cmp_w2"], wts["cmp_b2"], wts["k_g"]]
    in_specs = ([page_spec(k) for k in range(pg)]
                + [pl.BlockSpec(w.shape, functools.partial(lambda nd, n, g, pt: (0,) * nd, w.ndim)) for w in ws]
                + [pl.BlockSpec((1, 8, HD), lambda n, g, pt: (n, 0, 0)),
                   pl.BlockSpec(consts["imp"].shape, lambda n, g, pt: (0, 0))])
    kern = functools.partial(_scmp_kernel, pg=pg, n_groups=n_groups, layer=layer)
    return pl.pallas_call(
        kern,
        grid_spec=pltpu.PrefetchScalarGridSpec(
            num_scalar_prefetch=1,
            grid=(nb, n_groups),
            in_specs=in_specs,
            out_specs=[pl.BlockSpec((1, 8, HD), lambda n, g, pt: (n, 0, 0)),
                       pl.BlockSpec((1, 1, nsp), lambda n, g, pt: (n, 0, 0))],
            scratch_shapes=[pltpu.VMEM((CMP_STRIDE, pg * PAGE // CMP_STRIDE, 2 * HD), F32),
                            pltpu.VMEM((nc + 8, hid4), F32)],
        ),
        out_shape=[jax.ShapeDtypeStruct((nb, 8, HD), F32), jax.ShapeDtypeStruct((nb, 1, nsp), F32)],
        compiler_params=_cparams(("arbitrary", "arbitrary"), VMEM_BIG),
        name="sample_cmp",
    )(page_table, *([ckv_t] * pg), *ws, q8, consts["imp"])


def _stopk_kernel(p_ref, o_ref, *, cur, n_sel):
    p = p_ref[...]
    j_io = lax.broadcasted_iota(I32, p.shape, 1)
    j_f = j_io.astype(F32)
    validb = j_io <= cur
    forced = validb & ((j_io == 0) | (j_io > cur - 2))
    score = jnp.where(forced, jnp.inf, jnp.where(validb, p, -jnp.inf))
    selm = jnp.zeros(p.shape, F32)
    out = jnp.zeros(o_ref.shape, I32)
    o_io = lax.broadcasted_iota(I32, o_ref.shape, 1)
    for k in range(n_sel):
        free = selm == 0.0
        mx = jnp.max(jnp.where(free, score, -jnp.inf), axis=1, keepdims=True)
        hit = free & (score == mx)
        first = jnp.min(jnp.where(hit, j_f, float(p.shape[1])), axis=1, keepdims=True)
        selm = jnp.where(j_f == first, 1.0, selm)
        out = jnp.where(o_io == k, first.astype(I32), out)
    o_ref[...] = out


def _stopk(psel, cur, n_sel):
    nb = psel.shape[0]
    return _call(
        functools.partial(_stopk_kernel, cur=cur, n_sel=n_sel),
        grid=(1,),
        in_specs=[_full(psel.shape)],
        out_specs=_full((nb, LANES)),
        out_shape=jax.ShapeDtypeStruct((nb, LANES), I32),
        name="sample_topk",
    )(psel)


def _flash_rows(carry, s, vt):
    m, l, acc = carry
    mn = jnp.maximum(m, jnp.max(s, axis=1, keepdims=True))
    alpha = jnp.exp(m - mn)
    p = jnp.exp(s - mn)
    l = alpha * l + jnp.sum(p, axis=1, keepdims=True)
    acc = alpha * acc + _bdot_nt(p, vt)
    return mn, l, acc


def _bf(x):
    return x.astype(BF16).astype(F32)


def _row_to_col(row):
    n = row.shape[1]
    diag = lax.broadcasted_iota(I32, (n, n), 0) == lax.broadcasted_iota(I32, (n, n), 1)
    return jnp.sum(jnp.where(diag, jnp.broadcast_to(row, (n, n)), 0.0), axis=1, keepdims=True)


def _ssel_kernel(idx_ref, pt_ref, *refs, n_sel, cur, layer):
    del pt_ref, layer
    pages = refs[:n_sel]
    cw_ref, q_ref, new_ref, g_ref, oc_ref, o_ref, wo_ref = refs[n_sel:]
    n = pl.program_id(0)
    q8 = q_ref[0]
    new = new_ref[0]
    lane_half = lax.broadcasted_iota(I32, (8, PAGE), 1) >> SEL_SHIFT

    carry = (jnp.full((8, 1), 2 * NEG, F32), jnp.zeros((8, 1), F32), jnp.zeros((8, HD), F32))
    for k in range(n_sel):
        blk = idx_ref[n, k]
        s = _bdot(q8, pages[k][0, 0, 0])
        want = jnp.where(blk < cur, blk & 1, 2)
        carry = _flash_rows(carry, jnp.where(lane_half == want, s, NEG), pages[k][0, 0, 1])
    m, l, acc = carry
    s_new = jnp.sum(_bf(q8) * _bf(new[0:1, :]), axis=1, keepdims=True)
    mn = jnp.maximum(m, s_new)
    alpha = jnp.exp(m - mn)
    p_new = jnp.exp(s_new - mn)
    o_s = (alpha * acc + _bf(p_new) * _bf(new[1:2, :])) / (alpha * l + p_new)

    kw = cw_ref[0, 0, 0]
    vw = cw_ref[0, 0, 1]
    s_w = _bdot(q8, kw)
    s_wn = jnp.sum(_bf(q8) * _bf(new[2:3, :]), axis=1, keepdims=True)
    mw = jnp.maximum(jnp.max(s_w, axis=1, keepdims=True), s_wn)
    e = jnp.exp(s_w - mw)
    e_n = jnp.exp(s_wn - mw)
    o_w = (_bdot_nt(e, vw) + _bf(e_n) * _bf(new[3:4, :])) / (jnp.sum(e, axis=1, keepdims=True) + e_n)

    g = g_ref[0]
    o_ref[0] = g[:, 0:1] * oc_ref[0] + g[:, 1:2] * o_s + g[:, 2:3] * o_w

    wb = kw.shape[1]
    last = lax.broadcasted_iota(I32, (HD, wb), 1) == wb - 1
    wo_ref[0, 0, 0] = jnp.where(last, _row_to_col(new[2:3, :]), pltpu.roll(kw, wb - 1, axis=1))
    wo_ref[0, 0, 1] = jnp.where(last, _row_to_col(new[3:4, :]), pltpu.roll(vw, wb - 1, axis=1))


def _ssel(ckv_t, cw_t, idx, page_table, layer, q8, new8, g8, oc):
    nb, n_pages = page_table.shape
    n_sel = idx.shape[1]
    wb = cw_t.shape[-1]
    cur = n_pages * (PAGE // SEL_BLOCK)

    def page_spec(k):
        def imap(n, ix, pt):
            return (layer, pt[n, jnp.minimum(ix[n, k] // 2, n_pages - 1)], 1, 0, 0)
        return pl.BlockSpec((1, 1, 2, HD, PAGE), imap)

    in_specs = ([page_spec(k) for k in range(n_sel)]
                + [pl.BlockSpec((1, 1, 2, HD, wb), lambda n, ix, pt: (layer, n, 0, 0, 0)),
                   pl.BlockSpec((1, 8, HD), lambda n, ix, pt: (n, 0, 0)),
                   pl.BlockSpec((1, 8, HD), lambda n, ix, pt: (n, 0, 0)),
                   pl.BlockSpec((1, 8, 3), lambda n, ix, pt: (n, 0, 0)),
                   pl.BlockSpec((1, 8, HD), lambda n, ix, pt: (n, 0, 0))])
    kern = functools.partial(_ssel_kernel, n_sel=n_sel, cur=cur, layer=layer)
    return pl.pallas_call(
        kern,
        grid_spec=pltpu.PrefetchScalarGridSpec(
            num_scalar_prefetch=2,
            grid=(nb,),
            in_specs=in_specs,
            out_specs=[pl.BlockSpec((1, 8, HD), lambda n, ix, pt: (n, 0, 0)),
                       pl.BlockSpec((1, 1, 2, HD, wb), lambda n, ix, pt: (0, n, 0, 0, 0))],
        ),
        out_shape=[jax.ShapeDtypeStruct((nb, 8, HD), F32),
                   jax.ShapeDtypeStruct((1, nb, 2, HD, wb), F32)],
        compiler_params=_cparams(("arbitrary",)),
        name="sample_sel",
    )(idx, page_table, *([ckv_t] * n_sel), cw_t, q8, new8, g8, oc)


def _spost_kernel(x_ref, yabc_ref, sgd_ref, o_ref, ada_ref, w_ref, out_ref):
    y = jnp.concatenate([yabc_ref[...], sgd_ref[...] * o_ref[...]], axis=1)
    out_ref[...] = x_ref[...] + ada_ref[:, 2 * D_MODEL:3 * D_MODEL] * _bdot(y, w_ref[...])


def _spost(x, yabc, sgd, o, ada_rows, w):
    ins = [x, yabc, sgd, o, ada_rows, w]
    return _call(
        _spost_kernel,
        grid=(1,),
        in_specs=[_full(a.shape) for a in ins],
        out_specs=_full(x.shape),
        out_shape=jax.ShapeDtypeStruct(x.shape, F32),
        name="sample_post",
    )(*ins)


def _pad_cols(w, n):
    return jnp.concatenate([w, jnp.zeros(w.shape[:-1] + (n - w.shape[-1],), w.dtype)], axis=-1)


def _prep_w_in(w):
    o = np.cumsum([0, 256, 256, 256, 256, 256, 768, 4, 512, 256, 256, 384, 256, 12]).tolist()
    sec = [w[:, o[k]:o[k + 1]] for k in range(13)]
    a_h, a_b, a_c, a_g, b_z, b_xbc, b_dt, c_glu, c_g, d_q, d_kv, d_g, d_bg = sec
    cols = [a_h, a_b, a_c, a_g, b_z, b_xbc, _pad_cols(b_dt, LANES), c_glu, c_g,
            d_q, d_g, d_kv, _pad_cols(d_bg, LANES)]
    return jnp.concatenate(cols, axis=1).astype(BF16)


def _prep_wbd(w1):
    hid = w1.shape[-1]
    z = jnp.zeros((CMP_STRIDE, HD, hid), w1.dtype)
    top = jnp.concatenate([w1[0, :CMP_STRIDE], w1[0, CMP_STRIDE:], z, z], axis=-1)
    bot = jnp.concatenate([z, z, w1[1, :CMP_STRIDE], w1[1, CMP_STRIDE:]], axis=-1)
    return jnp.concatenate([top, bot], axis=1).astype(BF16)


def _layer_weights(l, a, pe_all):
    row = lambda v: v.reshape(1, -1)
    return {
        "sc_w": a["sc_w"][l], "bc_w": a["ssd_conv_w"][l], "bc_b": row(a["ssd_conv_b"][l]),
        "dtb128": _pad_cols(row(a["ssd_dt_bias"][l]), LANES),
        "alog128": _pad_cols(row(a["ssd_A_log"][l]), LANES),
        "alog256": row(jnp.repeat(a["ssd_A_log"][l], SSD_P)),
        "dvec": row(jnp.repeat(a["ssd_D"][l], SSD_P)),
        "ssd_ng": row(a["ssd_norm_g"][l]),
        "cf_w": a["cf_w"][l], "cf_b": row(a["cf_b"][l]),
        "ln_g": row(a["cf_ln_g"][l]), "ln_b": row(a["cf_ln_b"][l]),
        "q_g": row(jnp.tile(a["q_g"][l], NH)), "k_g": a["k_g"][l],
        "wbd": _prep_wbd(a["cmp_w1"][l]), "pe": pe_all[2 * l:2 * l + 2],
        "cmp_b1": a["cmp_b1"][l].reshape(2, 1, -1), "cmp_w2": a["cmp_w2"][l],
        "cmp_b2": a["cmp_b2"][l].reshape(2, 1, -1),
    }


def _sample_consts(n_pages):
    cpp = PAGE // CMP_STRIDE
    perm = np.zeros((PAGE, PAGE), np.float32)
    for j in range(CMP_STRIDE):
        for m in range(cpp):
            perm[cpp * j + m, CMP_STRIDE * m + j] = 1.0
    nc = n_pages * cpp
    ns = n_pages * (PAGE // SEL_BLOCK) + 1
    nsp = -(-ns // LANES) * LANES
    imp = np.zeros((nc, nsp), np.float32)
    r = SEL_BLOCK // CMP_STRIDE
    for j in range(ns):
        for o, wgt in ((-1, 0.5), (0, 1.0), (1, 1.0), (2, 1.0), (3, 0.5)):
            c = r * j + o
            if 0 <= c < nc:
                imp[c, j] = wgt
    return {"perm": jnp.asarray(perm, BF16), "imp": jnp.asarray(imp, BF16)}


def kernel(x_prompt, x_sample, cache_kv, cache_win, state_sconv, state_ssm_conv, state_ssm, state_cconv, page_table, c_prompt, c_sample, norm_g, w_ada, b_ada, w_in, w_out, sc_w, ssd_conv_w, ssd_conv_b, ssd_dt_bias, ssd_A_log, ssd_D, ssd_norm_g, cf_w, cf_b, cf_ln_g, cf_ln_b, q_g, k_g, cmp_pe, cmp_w1, cmp_b1, cmp_w2, cmp_b2):
    a = dict(sc_w=sc_w, ssd_conv_w=ssd_conv_w, ssd_conv_b=ssd_conv_b, ssd_dt_bias=ssd_dt_bias,
             ssd_A_log=ssd_A_log, ssd_D=ssd_D, ssd_norm_g=ssd_norm_g, cf_w=cf_w, cf_b=cf_b,
             cf_ln_g=cf_ln_g, cf_ln_b=cf_ln_b, q_g=q_g, k_g=k_g, cmp_w1=cmp_w1, cmp_b1=cmp_b1,
             cmp_w2=cmp_w2, cmp_b2=cmp_b2)
    depth = w_in.shape[0]
    assert x_prompt.shape[0] == 1 and x_sample.shape[1] == 1
    s = x_prompt.shape[1]
    nb = x_sample.shape[0]
    n_pages = page_table.shape[1]
    past = n_pages * PAGE
    assert s % 512 == 0 and s >= WINDOW + LANES and cache_win.shape[2] == min(WINDOW, past)

    n_c = 1 + nb
    c_all = jnp.concatenate([c_prompt, c_sample, jnp.zeros((-n_c % 8, D_MODEL), F32)], axis=0)
    ada = _ada(c_all, w_ada, b_ada)
    pe_all = _pe_terms(cmp_pe, cmp_w1)

    ckv_t = jnp.transpose(cache_kv, (0, 1, 3, 4, 2))
    cw_t = jnp.transpose(cache_win, (0, 1, 3, 4, 2))
    consts = _sample_consts(n_pages)

    xp = x_prompt[0]
    xs = x_sample[:, 0]
    outs_p, outs_s = [], []
    for l in range(depth):
        wts = _layer_weights(l, a, pe_all)
        w_in_l = _prep_w_in(w_in[l])
        w_out_l = w_out[l].astype(BF16)
        g_l = norm_g[l].reshape(1, -1)

        ada_p = ada[l, 0:1]
        pabc, pd = _inproj(xp, ada_p, g_l, w_in_l)
        yabc, sc_p, bc_p, hs_p, cf_p = _mix(pabc, wts)
        qT, kb, vT, kvp, winT, gT = _nsaprep(pd, wts["q_g"], wts["k_g"])
        kc, vcT = _pcompress(pd, wts)
        yd = _attn(qT, kb, vT, kc, vcT, gT, pd)
        xp = _outproj(xp, yabc, yd, ada_p, w_out_l)
        wlen = min(WINDOW, s)
        outs_p.append((jnp.transpose(kvp, (0, 3, 1, 2)),
                       jnp.transpose(winT[:, :, s - wlen:], (2, 0, 1))[None],
                       sc_p[None], bc_p[None], hs_p.reshape(1, SSD_H, SSD_P, SSD_S), cf_p[None]))

        ada_s = ada[l, 1:1 + nb]
        st = {"sc": jnp.transpose(state_sconv[l], (1, 0, 2)), "bc": jnp.transpose(state_ssm_conv[l], (1, 0, 2)),
              "hs": state_ssm[l].reshape(nb, SSD_H * SSD_P, SSD_S), "cf": jnp.transpose(state_cconv[l], (1, 0, 2))}
        (yabc_s, q_s, rows_s, win_s, gates_s, sgd_s, sc_s, bc_s, hs_s, cf_s) = _spre(xs, ada_s, g_l, w_in_l, st, wts)
        pad8 = lambda v: jnp.concatenate([v, jnp.zeros_like(v)], axis=1)
        q8 = pad8(q_s.reshape(nb, NH, HD))
        oc, psel = _scmp(ckv_t, page_table, l, q8, wts, consts)
        ns = past // SEL_BLOCK + 1
        idx = _stopk(psel.reshape(nb, -1), ns - 1, min(SEL_TOPN, ns))[:, :min(SEL_TOPN, ns)]
        new8 = pad8(jnp.concatenate([rows_s[:, 2 * HD:4 * HD], win_s], axis=1).reshape(nb, 4, HD))
        g8 = pad8(gates_s[:, 0:3 * NH].reshape(nb, NH, 3))
        o_s, win_new = _ssel(ckv_t, cw_t, idx, page_table, l, q8, new8, g8, oc)
        xs = _spost(xs, yabc_s, sgd_s, o_s[:, 0:NH].reshape(nb, WG), ada_s, w_out_l)
        outs_s.append((rows_s.reshape(nb, 1, 4, HD),
                       jnp.transpose(win_new[0], (0, 3, 1, 2)),
                       jnp.transpose(sc_s, (1, 0, 2)), jnp.transpose(bc_s, (1, 0, 2)),
                       hs_s.reshape(nb, SSD_H, SSD_P, SSD_S), jnp.transpose(cf_s, (1, 0, 2))))

    kv_p, win_p, sc_p, bc_p, h_p, cf_p = [jnp.stack(v) for v in zip(*outs_p)]
    kv_s, win_s, sc_s, bc_s, h_s, cf_s = [jnp.stack(v) for v in zip(*outs_s)]
    return (xp[None], xs[:, None], kv_p, kv_s, win_p, win_s, sc_p, sc_s, bc_p, bc_s, h_p, h_s, cf_p, cf_s)
```

```python
import functools

import numpy as np
import jax
import jax.numpy as jnp
from jax import lax
from jax.experimental import pallas as pl
from jax.experimental.pallas import tpu as pltpu

F32 = jnp.float32
BF16 = jnp.bfloat16
I32 = jnp.int32

D_MODEL = 1024
WG = 256
SC_K = 3
SSD_K = 4
SSD_CD = 768
SSD_H = 4
SSD_P = 64
SSD_S = 128
CF_K = 31
HD = 64
NH = 4
CMP_STRIDE = 16
SEL_BLOCK = 64
SEL_SHIFT = 6
SEL_TOPN = 16
SEL_LOCAL = 2
WINDOW = 512
PAGE = 128
EPS = 1e-6
NEG = -1e30
Q_SCALE = HD ** -0.5 * float(np.log2(np.e))
SAFE_EXP2 = 60.0
AUG = 64
HALO = 32
LANES = 128

N_ABC = 2944
N_D = 1024
N_IN = N_ABC + N_D
O_BZ, O_BX, O_BDT, O_CGLU, O_CG = 1024, 1280, 2048, 2176, 2688
O_DQ, O_DG, O_DKV, O_DBG = 0, 256, 512, 896

VMEM_BIG = 56 * 1024 * 1024


def _silu(x):
    return x * jax.nn.sigmoid(x)


def _softplus(x):
    return jnp.maximum(x, 0.0) + jnp.log1p(jnp.exp(-jnp.abs(x)))


def _bdot(a, b):
    return jnp.dot(a.astype(BF16), b.astype(BF16), preferred_element_type=F32)


def _bdot_nt(a, b):
    return lax.dot_general(a.astype(BF16), b.astype(BF16), (((1,), (1,)), ((), ())),
                           preferred_element_type=F32)


def _split3(a):
    a1 = a.astype(BF16)
    r1 = a - a1.astype(F32)
    a2 = r1.astype(BF16)
    a3 = (r1 - a2.astype(F32)).astype(BF16)
    return a1, a2, a3


def _dot_exact_rhs(a, m_bf16):
    a1, a2, a3 = _split3(a)
    d = functools.partial(jnp.dot, preferred_element_type=F32)
    return d(a1, m_bf16) + d(a2, m_bf16) + d(a3, m_bf16)


def _dot_exact_lhs(m_bf16, a):
    a1, a2, a3 = _split3(a)
    d = functools.partial(jnp.dot, preferred_element_type=F32)
    return d(m_bf16, a1) + d(m_bf16, a2) + d(m_bf16, a3)


def _group_ones(n, group):
    sh = group.bit_length() - 1
    assert group == 1 << sh
    r = lax.broadcasted_iota(I32, (n, n), 0) >> sh
    c = lax.broadcasted_iota(I32, (n, n), 1) >> sh
    return jnp.where(r == c, 1.0, 0.0).astype(BF16)


def _group_rms(x, group):
    ssq = _dot_exact_rhs(x * x, _group_ones(x.shape[-1], group))
    return x * lax.rsqrt(ssq * (1.0 / group) + EPS)


def _rms(x):
    return x * lax.rsqrt(jnp.mean(x * x, axis=-1, keepdims=True) + EPS)


def _modulated_norm(x, g, scale, shift):
    return _rms(x) * g * (1.0 + scale) + shift


def _call(kernel, **kw):
    return pl.pallas_call(kernel, **kw)


def _cparams(sem=None, vmem=None):
    kw = {}
    if sem is not None:
        kw["dimension_semantics"] = sem
    if vmem is not None:
        kw["vmem_limit_bytes"] = vmem
    return pltpu.CompilerParams(**kw)


def _full(shape):
    nd = len(shape)
    return pl.BlockSpec(shape, lambda *_: (0,) * nd)


def _ada_kernel(c_ref, w_ref, b_ref, o_ref):
    o_ref[0] = _bdot(_silu(c_ref[...]), w_ref[0]) + b_ref[0]


def _ada(c_all, w_ada, b_ada):
    depth = w_ada.shape[0]
    rows = c_all.shape[0]
    tn = 512
    return _call(
        _ada_kernel,
        grid=(depth, 3 * D_MODEL // tn),
        in_specs=[pl.BlockSpec((rows, D_MODEL), lambda l, j: (0, 0)),
                  pl.BlockSpec((1, D_MODEL, tn), lambda l, j: (l, 0, j)),
                  pl.BlockSpec((1, 1, tn), lambda l, j: (l, 0, j))],
        out_specs=pl.BlockSpec((1, rows, tn), lambda l, j: (l, 0, j)),
        out_shape=jax.ShapeDtypeStruct((depth, rows, 3 * D_MODEL), F32),
        name="ada",
    )(c_all, w_ada, b_ada.reshape(depth, 1, 3 * D_MODEL))


def _pe_kernel(pe_ref, w_ref, o_ref):
    o_ref[0] = jnp.sum(pe_ref[0] * w_ref[0], axis=0, keepdims=True)


def _pe_terms(cmp_pe, cmp_w1):
    depth = cmp_pe.shape[0]
    n = depth * 2
    kk = cmp_pe.shape[2] * cmp_pe.shape[3]
    hid = cmp_w1.shape[-1]
    return _call(
        _pe_kernel,
        grid=(n,),
        in_specs=[pl.BlockSpec((1, kk, 1), lambda i: (i, 0, 0)),
                  pl.BlockSpec((1, kk, hid), lambda i: (i, 0, 0))],
        out_specs=pl.BlockSpec((1, 1, hid), lambda i: (i, 0, 0)),
        out_shape=jax.ShapeDtypeStruct((n, 1, hid), F32),
        name="pe_term",
    )(cmp_pe.reshape(n, kk, 1), cmp_w1.reshape(n, kk, hid))


def _inproj_kernel(x_ref, ada_ref, g_ref, w_ref, oabc_ref, od_ref):
    shift = ada_ref[:, 0:D_MODEL]
    scale = ada_ref[:, D_MODEL:2 * D_MODEL]
    h = _modulated_norm(x_ref[...], g_ref[...], scale, shift)
    p = _bdot(h, w_ref[...])
    oabc_ref[...] = p[:, :N_ABC]
    od_ref[...] = p[:, N_ABC:]


def _inproj(x, ada_row, g, w):
    s = x.shape[0]
    tr = 256
    return _call(
        _inproj_kernel,
        grid=(s // tr,),
        in_specs=[pl.BlockSpec((tr, D_MODEL), lambda i: (i, 0)),
                  _full((1, 3 * D_MODEL)), _full((1, D_MODEL)), _full((D_MODEL, N_IN))],
        out_specs=[pl.BlockSpec((tr, N_ABC), lambda i: (i, 0)),
                   pl.BlockSpec((tr, N_D), lambda i: (i, 0))],
        out_shape=[jax.ShapeDtypeStruct((s, N_ABC), F32), jax.ShapeDtypeStruct((s, N_D), F32)],
        compiler_params=_cparams(("arbitrary",), VMEM_BIG),
        name="inproj",
    )(x, ada_row, g, w)


def _tile_conv(ext_ref, u, w_ref, taps, t):
    ext_ref[HALO:HALO + t, :] = u
    acc = None
    for k in range(taps):
        o = HALO - (taps - 1) + k
        term = w_ref[k:k + 1, :] * ext_ref[o:o + t, :]
        acc = term if acc is None else acc + term
    return acc


def _ssd_chunk(xs, bm, cm, dtp, a, hst_ref):
    t = xs.shape[0]
    ri = lax.broadcasted_iota(I32, (t, t), 0)
    ci = lax.broadcasted_iota(I32, (t, t), 1)
    tril = ri >= ci
    trilb = jnp.where(tril, 1.0, 0.0).astype(BF16)
    triub = jnp.where(ri <= ci, 1.0, 0.0).astype(BF16)
    acs_col = _dot_exact_lhs(trilb, a)
    acs_row = _dot_exact_rhs(a.T, triub)
    ys = []
    for g in range(2):
        bg = bm[:, SSD_S * g:SSD_S * (g + 1)]
        cg = cm[:, SSD_S * g:SSD_S * (g + 1)]
        gram = _bdot_nt(cg, bg)
        xte = []
        for hh in range(2):
            h = 2 * g + hh
            ac = acs_col[:, h:h + 1]
            ar = acs_row[h:h + 1, :]
            a_last = acs_col[t - 1:t, h:h + 1]
            decay = jnp.where(tril, jnp.exp(jnp.where(tril, ac - ar, 0.0)), 0.0)
            xdt = xs[:, SSD_P * h:SSD_P * (h + 1)] * dtp[:, h:h + 1]
            y_diag = _bdot(gram * decay, xdt)
            hprev = hst_ref[SSD_P * h:SSD_P * (h + 1), :]
            y_off = _bdot_nt(cg, hprev) * jnp.exp(ac)
            ys.append(y_diag + y_off)
            xte.append(xdt * jnp.exp(a_last - ac))
        x2t = jnp.concatenate(xte, axis=1).T
        states = _bdot(x2t, bg)
        for hh in range(2):
            h = 2 * g + hh
            cd = jnp.exp(acs_col[t - 1:t, h:h + 1])
            sl = slice(SSD_P * h, SSD_P * (h + 1))
            hst_ref[sl, :] = cd * hst_ref[sl, :] + states[SSD_P * hh:SSD_P * (hh + 1), :]
    return jnp.concatenate(ys, axis=1)


def _mix_kernel(p_ref, scw_ref, bcw_ref, bcb_ref, dtb_ref, alog_ref, dvec_ref, ng_ref,
                cfw_ref, cfb_ref, lng_ref, lnb_ref,
                y_ref, sc_ref, bc_ref, hs_ref, cf_ref,
                exta, extb, extc, hst):
    t = p_ref.shape[0]
    i = pl.program_id(0)

    @pl.when(i == 0)
    def _():
        exta[0:HALO, :] = jnp.zeros((HALO, WG), F32)
        extb[0:HALO, :] = jnp.zeros((HALO, SSD_CD), F32)
        extc[0:HALO, :] = jnp.zeros((HALO, WG), F32)
        hst[...] = jnp.zeros_like(hst)

    u_a = p_ref[:, 512:768] * p_ref[:, 0:256]
    conv_a = _tile_conv(exta, u_a, scw_ref, SC_K, t)
    y_a = _silu(p_ref[:, 768:1024]) * p_ref[:, 256:512] * conv_a

    conv_b = _tile_conv(extb, p_ref[:, O_BX:O_BX + SSD_CD], bcw_ref, SSD_K, t) + bcb_ref[...]
    xbc = _silu(conv_b)
    xs = xbc[:, 0:WG]
    dtp = _softplus(p_ref[:, O_BDT:O_BDT + LANES] + dtb_ref[...])
    a = dtp * (-jnp.exp(alog_ref[...]))
    y_ssd = _ssd_chunk(xs, xbc[:, WG:WG + 2 * SSD_S], xbc[:, WG + 2 * SSD_S:], dtp, a, hst)
    y_b = y_ssd + dvec_ref[...] * xs
    y_b = _rms(y_b * _silu(p_ref[:, O_BZ:O_BZ + WG])) * ng_ref[...]

    glu = p_ref[:, O_CGLU:O_CGLU + WG] * jax.nn.sigmoid(p_ref[:, O_CGLU + WG:O_CGLU + 2 * WG])
    u_c = _tile_conv(extc, glu, cfw_ref, CF_K, t) + cfb_ref[...]
    mu = jnp.mean(u_c, axis=-1, keepdims=True)
    var = jnp.mean(jnp.square(u_c - mu), axis=-1, keepdims=True)
    ln = (u_c - mu) * lax.rsqrt(var + EPS) * lng_ref[...] + lnb_ref[...]
    y_c = _silu(p_ref[:, O_CG:O_CG + WG]) * _silu(ln)

    y_ref[:, 0:WG] = y_a
    y_ref[:, WG:2 * WG] = y_b
    y_ref[:, 2 * WG:3 * WG] = y_c

    sc_ref[...] = exta[HALO + t - (SC_K - 1):HALO + t, :]
    bc_ref[...] = extb[HALO + t - (SSD_K - 1):HALO + t, :]
    cf_ref[...] = extc[HALO + t - (CF_K - 1):HALO + t, :]
    hs_ref[...] = hst[...]
    exta[0:HALO, :] = exta[t:t + HALO, :]
    extb[0:HALO, :] = extb[t:t + HALO, :]
    extc[0:HALO, :] = extc[t:t + HALO, :]


def _mix(pabc, wts):
    s = pabc.shape[0]
    t = SSD_S
    small = [wts[k] for k in ("sc_w", "bc_w", "bc_b", "dtb128", "alog128", "dvec", "ssd_ng",
                              "cf_w", "cf_b", "ln_g", "ln_b")]
    return _call(
        _mix_kernel,
        grid=(s // t,),
        in_specs=[pl.BlockSpec((t, N_ABC), lambda i: (i, 0))] + [_full(w.shape) for w in small],
        out_specs=[pl.BlockSpec((t, 3 * WG), lambda i: (i, 0)),
                   _full((SC_K - 1, WG)), _full((SSD_K - 1, SSD_CD)),
                   _full((SSD_H * SSD_P, SSD_S)), _full((CF_K - 1, WG))],
        out_shape=[jax.ShapeDtypeStruct((s, 3 * WG), F32),
                   jax.ShapeDtypeStruct((SC_K - 1, WG), F32),
                   jax.ShapeDtypeStruct((SSD_K - 1, SSD_CD), F32),
                   jax.ShapeDtypeStruct((SSD_H * SSD_P, SSD_S), F32),
                   jax.ShapeDtypeStruct((CF_K - 1, WG), F32)],
        scratch_shapes=[pltpu.VMEM((HALO + t, WG), F32), pltpu.VMEM((HALO + t, SSD_CD), F32),
                        pltpu.VMEM((HALO + t, WG), F32), pltpu.VMEM((SSD_H * SSD_P, SSD_S), F32)],
        compiler_params=_cparams(("arbitrary",)),
        name="mix",
    )(pabc, *small)


def _nsa_rows(pd, qg, kg1, kg2):
    q = _group_rms(pd[:, O_DQ:O_DQ + WG], HD) * qg * Q_SCALE
    kv = pd[:, O_DKV:O_DKV + 6 * HD]
    k_slc = _rms(kv[:, 2 * HD:3 * HD]) * kg1
    k_win = _rms(kv[:, 4 * HD:5 * HD]) * kg2
    rows = jnp.concatenate([kv[:, 0:2 * HD], k_slc, kv[:, 3 * HD:4 * HD]], axis=1)
    win = jnp.concatenate([k_win, kv[:, 5 * HD:6 * HD]], axis=1)
    gates = jax.nn.sigmoid(pd[:, O_DBG:O_DBG + LANES])
    return q, rows, win, gates


def _row_norm_max(k):
    kb = k.astype(BF16).astype(F32)
    return jnp.sqrt(jnp.max(jnp.sum(kb * kb, axis=1, keepdims=True), axis=0, keepdims=True))


def _nsaprep_kernel(pd_ref, qg_ref, kg_ref, qT_ref, ks_ref, kw_ref, vT_ref, kvp_ref, winT_ref, gT_ref, kmax_ref):
    tr = pd_ref.shape[0]
    i = pl.program_id(0)
    q, rows, win, gates = _nsa_rows(pd_ref[...], qg_ref[...], kg_ref[1:2, :], kg_ref[2:3, :])
    qT_ref[...] = q.T.astype(BF16)
    k_slc = rows[:, 2 * HD:3 * HD]
    k_win = win[:, 0:HD]
    col = lax.broadcasted_iota(I32, (tr, AUG), 1)
    blk = ((i * tr + lax.broadcasted_iota(I32, (tr, AUG), 0)) >> SEL_SHIFT) & (AUG // 2 - 1)
    ones_col = col == AUG // 2
    ks_ref[...] = jnp.concatenate([k_slc, jnp.where((col == blk) | ones_col, 1.0, 0.0)], axis=1).astype(BF16)
    kw_ref[...] = jnp.concatenate([k_win, jnp.where(ones_col, 1.0, 0.0)], axis=1).astype(BF16)
    kmax = jnp.broadcast_to(_row_norm_max(k_slc), (8, LANES))

    @pl.when(i == 0)
    def _():
        kmax_ref[...] = kmax

    @pl.when(i > 0)
    def _():
        kmax_ref[...] = jnp.maximum(kmax_ref[...], kmax)

    rows_t = rows.T
    win_t = win.T
    vT_ref[...] = jnp.concatenate([rows_t[3 * HD:4 * HD, :], win_t[HD:2 * HD, :]], axis=0).astype(BF16)
    for p in range(tr // PAGE):
        kvp_ref[p] = rows_t[:, PAGE * p:PAGE * (p + 1)].reshape(4, HD, PAGE)
    winT_ref[...] = win_t.reshape(2, HD, tr)
    gT_ref[...] = gates.T[0:16, :]


def _nsaprep(pd, qg, kg):
    s = pd.shape[0]
    tr = 256
    return _call(
        _nsaprep_kernel,
        grid=(s // tr,),
        in_specs=[pl.BlockSpec((tr, N_D), lambda i: (i, 0)), _full(qg.shape), _full(kg.shape)],
        out_specs=[pl.BlockSpec((WG, tr), lambda i: (0, i)),
                   pl.BlockSpec((tr, 2 * HD), lambda i: (i, 0)),
                   pl.BlockSpec((tr, 2 * HD), lambda i: (i, 0)),
                   pl.BlockSpec((2 * HD, tr), lambda i: (0, i)),
                   pl.BlockSpec((tr // PAGE, 4, HD, PAGE), lambda i: (i, 0, 0, 0)),
                   pl.BlockSpec((2, HD, tr), lambda i: (0, 0, i)),
                   pl.BlockSpec((16, tr), lambda i: (0, i)),
                   _full((8, LANES))],
        out_shape=[jax.ShapeDtypeStruct((WG, s), BF16),
                   jax.ShapeDtypeStruct((s, 2 * HD), BF16),
                   jax.ShapeDtypeStruct((s, 2 * HD), BF16),
                   jax.ShapeDtypeStruct((2 * HD, s), BF16),
                   jax.ShapeDtypeStruct((s // PAGE, 4, HD, PAGE), F32),
                   jax.ShapeDtypeStruct((2, HD, s), F32),
                   jax.ShapeDtypeStruct((16, s), F32),
                   jax.ShapeDtypeStruct((8, LANES), F32)],
        compiler_params=_cparams(("arbitrary",)),
        name="nsaprep",
    )(pd, qg, kg)


def _cmp_tail(pre_ref, nc, pe_ref, b1_ref, w2_ref, b2_ref, kg0):
    hid = b1_ref.shape[-1]
    outs = []
    for s in range(2):
        first = pre_ref[0:nc, 2 * s * hid:(2 * s + 1) * hid]
        second = pre_ref[pl.ds(1, nc), (2 * s + 1) * hid:(2 * s + 2) * hid]
        h = _silu(first + second + pe_ref[s] + b1_ref[s])
        outs.append(_bdot(h, w2_ref[s]) + b2_ref[s])
    return _rms(outs[0]) * kg0, outs[1]


def _pcompress_kernel(kv_ref, wbd_ref, pe_ref, b1_ref, w2_ref, b2_ref, kg_ref, kc_ref, vcT_ref, pre):
    nc = kv_ref.shape[0] // CMP_STRIDE
    acc = None
    for j in range(CMP_STRIDE):
        term = _bdot(kv_ref[pl.ds(j, nc, stride=CMP_STRIDE), :], wbd_ref[j])
        acc = term if acc is None else acc + term
    pre[0:nc, :] = acc
    pre[nc:nc + 8, :] = jnp.zeros((8, pre.shape[1]), F32)
    kc, vc = _cmp_tail(pre, nc, pe_ref, b1_ref, w2_ref, b2_ref, kg_ref[0:1, :])
    kc_ref[...] = kc.astype(BF16)
    vcT_ref[...] = jnp.concatenate([vc, jnp.zeros_like(vc)], axis=1).T[0:HD, :].astype(BF16)


def _pcompress(pd, wts):
    s = pd.shape[0]
    nc = s // CMP_STRIDE
    ws = [wts[k] for k in ("wbd", "pe", "cmp_b1", "cmp_w2", "cmp_b2", "k_g")]
    return _call(
        _pcompress_kernel,
        grid=(1,),
        in_specs=[pl.BlockSpec((s, 2 * HD), lambda i: (0, O_DKV // (2 * HD)))] + [_full(w.shape) for w in ws],
        out_specs=[_full((nc, HD)), _full((HD, nc))],
        out_shape=[jax.ShapeDtypeStruct((nc, HD), BF16), jax.ShapeDtypeStruct((HD, nc), BF16)],
        scratch_shapes=[pltpu.VMEM((nc + 8, 4 * wts["cmp_b1"].shape[-1]), F32)],
        compiler_params=_cparams(("arbitrary",), VMEM_BIG),
        name="pcompress",
    )(pd, *ws)


def _masked_softmax_cols(s, mask):
    sm = jnp.where(mask, s, NEG)
    m = jnp.max(sm, axis=0, keepdims=True)
    e = jnp.where(mask, jnp.exp2(sm - m), 0.0)
    l = jnp.sum(e, axis=0, keepdims=True)
    return e * jnp.where(l > 0.0, 1.0 / l, 0.0)


def _flash_cols(carry, s, vt):
    m, l, acc = carry
    mn = jnp.maximum(m, jnp.max(s, axis=0, keepdims=True))
    alpha = jnp.exp2(m - mn)
    p = jnp.exp2(s - mn)
    l = alpha * l + jnp.sum(p, axis=0, keepdims=True)
    acc = alpha * acc + jnp.dot(vt, p.astype(BF16), preferred_element_type=F32)
    return mn, l, acc


def _attn_kernel(qT_ref, ks_ref, kw_ref, vT_ref, kc_ref, vcT_ref, gT_ref, dg_ref, kmax_ref,
                 o_ref, pg_scr, bias_scr, *, n_sel, tk):
    qb = qT_ref.shape[1]
    nq = NH * qb
    half = AUG // 2
    i = pl.program_id(0)
    q0 = i * qb
    qt = qT_ref[...]
    qall = jnp.concatenate([qt[HD * h:HD * (h + 1), :] for h in range(NH)], axis=1)

    def q_aug(extra):
        return jnp.concatenate([qall, extra.astype(BF16)], axis=0)

    qf = qall.astype(F32)
    qn = jnp.sqrt(jnp.sum(qf * qf, axis=0, keepdims=True))
    b_s = qn * kmax_ref[0:1, 0:1]
    fast = jnp.max(b_s) <= SAFE_EXP2
    row_a = lax.broadcasted_iota(I32, (half, nq), 0)
    zero_half = jnp.zeros((half, nq), F32)

    def stab_rows(b):
        return jnp.where(row_a == 0, -b, 0.0)

    nc = kc_ref.shape[0]
    s_c = jnp.dot(kc_ref[...], qall, preferred_element_type=F32)
    c_io = lax.broadcasted_iota(I32, (nc, nq), 0)
    qpos = q0 + (lax.broadcasted_iota(I32, (nc, nq), 1) & (qb - 1))
    p_c = _masked_softmax_cols(s_c, c_io * CMP_STRIDE + (2 * CMP_STRIDE - 1) <= qpos)
    o_c = jnp.dot(vcT_ref[...], p_c.astype(BF16), preferred_element_type=F32)

    wk = WINDOW + qb
    start = pl.multiple_of(jnp.maximum(q0 - WINDOW, 0), qb)
    s_w = jnp.dot(kw_ref[pl.ds(start, wk), :], q_aug(jnp.zeros((AUG, nq), F32)),
                  preferred_element_type=F32)
    dpos = (q0 + (lax.broadcasted_iota(I32, (wk, nq), 1) & (qb - 1))
            - (start + lax.broadcasted_iota(I32, (wk, nq), 0)))
    p_w = _masked_softmax_cols(s_w, (dpos >= 0) & (dpos <= WINDOW))
    o_w = jnp.dot(vT_ref[HD:2 * HD, pl.ds(start, wk)], p_w.astype(BF16), preferred_element_type=F32)

    pg = p_c[:, 0:qb]
    for h in range(1, NH):
        pg = pg + p_c[:, qb * h:qb * (h + 1)]
    pg_scr[0:8, :] = jnp.zeros((8, qb), F32)
    pg_scr[8:8 + nc, :] = pg
    pg_scr[8 + nc:16 + nc, :] = jnp.zeros((8, qb), F32)
    ns = nc // 4

    def strided(k):
        return pg_scr[pl.ds(8 + k, ns, stride=4), :]

    p_sel = 0.5 * strided(-1) + strided(0)
    p_sel = p_sel + strided(1)
    p_sel = p_sel + strided(2)
    p_sel = p_sel + 0.5 * strided(3)

    j_io = lax.broadcasted_iota(I32, (ns, qb), 0)
    j_f = j_io.astype(F32)
    cur = (q0 + lax.broadcasted_iota(I32, (ns, qb), 1)) >> SEL_SHIFT
    validb = j_io <= cur
    forced = validb & ((j_io == 0) | (j_io > cur - SEL_LOCAL))
    n_forced = 1 + SEL_LOCAL
    assert n_sel > n_forced
    open0 = validb & jnp.logical_not(forced)
    score = jnp.where(open0, p_sel, -jnp.inf)
    for _ in range(n_sel - n_forced):
        mx = jnp.max(score, axis=0, keepdims=True)
        first = jnp.min(jnp.where(score == mx, j_f, float(ns)), axis=0, keepdims=True)
        score = jnp.where(j_f == first, -jnp.inf, score)
    picked = forced | (open0 & (score == -jnp.inf))
    older = j_io < (q0 >> SEL_SHIFT)
    bias_scr[...] = jnp.where(picked & older, 0.0, NEG)

    assert tk == half * SEL_BLOCK

    def sel_scores(t, stab):
        kt = ks_ref[pl.ds(pl.multiple_of(t * tk, tk), tk), :]
        bt = bias_scr[pl.ds(pl.multiple_of(t * half, half), half), :]
        extra = jnp.concatenate([jnp.concatenate([bt] * NH, axis=1), stab], axis=0)
        return jnp.dot(kt, q_aug(extra), preferred_element_type=F32)

    def v_slc(t):
        return vT_ref[0:HD, pl.ds(pl.multiple_of(t * tk, tk), tk)]

    n_tiles = q0 // tk + 1
    k_own = ks_ref[pl.ds(pl.multiple_of(q0, qb), qb), :]
    v_own = vT_ref[0:HD, pl.ds(pl.multiple_of(q0, qb), qb)]
    causal = (lax.broadcasted_iota(I32, (qb, nq), 0)
              <= (lax.broadcasted_iota(I32, (qb, nq), 1) & (qb - 1)))

    def own_scores(stab):
        return jnp.where(causal, jnp.dot(k_own, q_aug(jnp.concatenate([zero_half, stab], axis=0)),
                                         preferred_element_type=F32), NEG)

    def sel_fast():
        stab = stab_rows(b_s)

        def step(l, acc, s, vt):
            p = jnp.exp2(s)
            return (l + jnp.sum(p, axis=0, keepdims=True),
                    acc + jnp.dot(vt, p.astype(BF16), preferred_element_type=F32))

        l, acc = lax.fori_loop(0, n_tiles, lambda t, c: step(c[0], c[1], sel_scores(t, stab), v_slc(t)),
                               (jnp.zeros((1, nq), F32), jnp.zeros((HD, nq), F32)))
        l, acc = step(l, acc, own_scores(stab), v_own)
        return acc / l

    def sel_slow():
        init = (jnp.full((1, nq), 2 * NEG, F32), jnp.zeros((1, nq), F32), jnp.zeros((HD, nq), F32))
        carry = lax.fori_loop(0, n_tiles, lambda t, c: _flash_cols(c, sel_scores(t, zero_half), v_slc(t)), init)
        _, l, acc = _flash_cols(carry, own_scores(zero_half), v_own)
        return acc / l

    o_s = lax.cond(fast, sel_fast, sel_slow)

    heads = []
    for h in range(NH):
        sl = slice(qb * h, qb * (h + 1))
        heads.append(gT_ref[3 * h:3 * h + 1, :] * o_c[:, sl] + gT_ref[3 * h + 1:3 * h + 2, :] * o_s[:, sl]
                     + gT_ref[3 * h + 2:3 * h + 3, :] * o_w[:, sl])
    o_rows = jnp.concatenate(heads, axis=0).T
    o_ref[...] = _silu(dg_ref[...]) * o_rows


def _attn(qT, ks, kw, vT, kc, vcT, gT, pd, kmax):
    s = ks.shape[0]
    qb = LANES
    nc = kc.shape[0]
    tk = (AUG // 2) * SEL_BLOCK
    kern = functools.partial(_attn_kernel, n_sel=min(SEL_TOPN, s // SEL_BLOCK), tk=tk)
    return _call(
        kern,
        grid=(s // qb,),
        in_specs=[pl.BlockSpec((WG, qb), lambda i: (0, i)),
                  _full(ks.shape), _full(kw.shape), _full(vT.shape), _full(kc.shape), _full(vcT.shape),
                  pl.BlockSpec((16, qb), lambda i: (0, i)),
                  pl.BlockSpec((qb, WG), lambda i: (i, O_DG // WG)),
                  _full(kmax.shape)],
        out_specs=pl.BlockSpec((qb, WG), lambda i: (i, 0)),
        out_shape=jax.ShapeDtypeStruct((s, WG), F32),
        scratch_shapes=[pltpu.VMEM((nc + 16, qb), F32), pltpu.VMEM((nc // 4, qb), F32)],
        compiler_params=_cparams(("arbitrary",), VMEM_BIG),
        name="attn",
    )(qT, ks, kw, vT, kc, vcT, gT, pd, kmax)


def _outproj_kernel(x_ref, yabc_ref, yd_ref, ada_ref, w_ref, o_ref):
    y = jnp.concatenate([yabc_ref[...], yd_ref[...]], axis=1)
    o_ref[...] = x_ref[...] + ada_ref[:, 2 * D_MODEL:3 * D_MODEL] * _bdot(y, w_ref[...])


def _outproj(x, yabc, yd, ada_rows, w):
    s = x.shape[0]
    tr = min(512, s)
    ar = ada_rows.shape[0]
    ada_spec = (_full((1, 3 * D_MODEL)) if ar == 1 else pl.BlockSpec((tr, 3 * D_MODEL), lambda i: (i, 0)))
    return _call(
        _outproj_kernel,
        grid=(s // tr,),
        in_specs=[pl.BlockSpec((tr, D_MODEL), lambda i: (i, 0)),
                  pl.BlockSpec((tr, 3 * WG), lambda i: (i, 0)),
                  pl.BlockSpec((tr, WG), lambda i: (i, 0)),
                  ada_spec, _full((4 * WG, D_MODEL))],
        out_specs=pl.BlockSpec((tr, D_MODEL), lambda i: (i, 0)),
        out_shape=jax.ShapeDtypeStruct((s, D_MODEL), F32),
        compiler_params=_cparams(("arbitrary",)),
        name="outproj",
    )(x, yabc, yd, ada_rows, w)


def _pad_t(scr, x):
    nb = x.shape[0]
    scr[...] = jnp.zeros_like(scr)
    scr[0:nb, :] = x
    return scr[...].T


def _spre_kernel(x_ref, ada_ref, g_ref, w_ref, sc_ref, bc_ref, hs_ref, cf_ref,
                 scw_ref, bcw_ref, bcb_ref, dtb_ref, alog_ref, dvec_ref, ng_ref,
                 cfw_ref, cfb_ref, lng_ref, lnb_ref, qg_ref, kg_ref,
                 y_ref, q_ref, rows_ref, win_ref, gates_ref, sgd_ref,
                 sco_ref, bco_ref, hso_ref, cfo_ref,
                 xbc_scr, padA, padB):
    nb = x_ref.shape[0]
    h = _modulated_norm(x_ref[...], g_ref[...], ada_ref[:, D_MODEL:2 * D_MODEL], ada_ref[:, 0:D_MODEL])
    p = _bdot(h, w_ref[...])

    u_a = p[:, 512:768] * p[:, 0:256]
    conv_a = scw_ref[0:1, :] * sc_ref[0] + scw_ref[1:2, :] * sc_ref[1] + scw_ref[2:3, :] * u_a
    y_ref[:, 0:WG] = _silu(p[:, 768:1024]) * p[:, 256:512] * conv_a
    sco_ref[0] = sc_ref[1]
    sco_ref[1] = u_a

    bx = p[:, O_BX:O_BX + SSD_CD]
    conv_b = (bcw_ref[0:1, :] * bc_ref[0] + bcw_ref[1:2, :] * bc_ref[1] + bcw_ref[2:3, :] * bc_ref[2]
              + bcw_ref[3:4, :] * bx + bcb_ref[...])
    bco_ref[0] = bc_ref[1]
    bco_ref[1] = bc_ref[2]
    bco_ref[2] = bx
    xbc = _silu(conv_b)
    xbc_scr[...] = xbc
    xs = xbc[:, 0:WG]
    dtp = _softplus(p[:, O_BDT:O_BDT + LANES] + dtb_ref[...])
    dt_c = jnp.concatenate([jnp.broadcast_to(dtp[:, hh:hh + 1], (nb, SSD_P)) for hh in range(SSD_H)], axis=1)
    dec_t = _pad_t(padA, jnp.exp(dt_c * (-jnp.exp(alog_ref[...]))))
    xdt_t = _pad_t(padB, xs * dt_c)
    lane = lax.broadcasted_iota(I32, (SSD_H * SSD_P, LANES), 1)

    def sample_step(n, ymat):
        oh = lane == n
        dcol = jnp.sum(jnp.where(oh, dec_t, 0.0), axis=1, keepdims=True)
        xcol = jnp.sum(jnp.where(oh, xdt_t, 0.0), axis=1, keepdims=True)
        brow = xbc_scr[pl.ds(n, 1), WG:WG + 2 * SSD_S]
        crow = xbc_scr[pl.ds(n, 1), WG + 2 * SSD_S:WG + 4 * SSD_S]
        half = 2 * SSD_P
        bfull = jnp.concatenate([jnp.broadcast_to(brow[:, 0:SSD_S], (half, SSD_S)),
                                 jnp.broadcast_to(brow[:, SSD_S:], (half, SSD_S))], axis=0)
        cfull = jnp.concatenate([jnp.broadcast_to(crow[:, 0:SSD_S], (half, SSD_S)),
                                 jnp.broadcast_to(crow[:, SSD_S:], (half, SSD_S))], axis=0)
        hn = dcol * hs_ref[n] + xcol * bfull
        hso_ref[n] = hn
        ycol = jnp.sum(hn * cfull, axis=1, keepdims=True)
        return jnp.where(oh, ycol, ymat)

    ymat = lax.fori_loop(0, nb, sample_step, jnp.zeros((SSD_H * SSD_P, LANES), F32))
    y_b = ymat.T[0:nb, :] + dvec_ref[...] * xs
    y_ref[:, WG:2 * WG] = _rms(y_b * _silu(p[:, O_BZ:O_BZ + WG])) * ng_ref[...]

    glu = p[:, O_CGLU:O_CGLU + WG] * jax.nn.sigmoid(p[:, O_CGLU + WG:O_CGLU + 2 * WG])
    u_c = cfw_ref[CF_K - 1:CF_K, :] * glu + cfb_ref[...]
    for k in range(CF_K - 1):
        u_c = u_c + cfw_ref[k:k + 1, :] * cf_ref[k]
    for k in range(CF_K - 2):
        cfo_ref[k] = cf_ref[k + 1]
    cfo_ref[CF_K - 2] = glu
    mu = jnp.mean(u_c, axis=-1, keepdims=True)
    var = jnp.mean(jnp.square(u_c - mu), axis=-1, keepdims=True)
    ln = (u_c - mu) * lax.rsqrt(var + EPS) * lng_ref[...] + lnb_ref[...]
    y_ref[:, 2 * WG:3 * WG] = _silu(p[:, O_CG:O_CG + WG]) * _silu(ln)

    pd = p[:, N_ABC:]
    q, rows, win, gates = _nsa_rows(pd, qg_ref[...], kg_ref[1:2, :], kg_ref[2:3, :])
    q_ref[...] = q
    rows_ref[...] = rows
    win_ref[...] = win
    gates_ref[...] = gates
    sgd_ref[...] = _silu(pd[:, O_DG:O_DG + WG])


def _spre(x, ada_rows, g, w, st, wts):
    nb = x.shape[0]
    small = [wts[k] for k in ("sc_w", "bc_w", "bc_b", "dtb128", "alog256", "dvec", "ssd_ng",
                              "cf_w", "cf_b", "ln_g", "ln_b", "q_g", "k_g")]
    ins = [x, ada_rows, g, w, st["sc"], st["bc"], st["hs"], st["cf"]] + small
    outs = [jax.ShapeDtypeStruct((nb, 3 * WG), F32),
            jax.ShapeDtypeStruct((nb, WG), F32),
            jax.ShapeDtypeStruct((nb, WG), F32),
            jax.ShapeDtypeStruct((nb, 2 * HD), F32),
            jax.ShapeDtypeStruct((nb, LANES), F32),
            jax.ShapeDtypeStruct((nb, WG), F32),
            jax.ShapeDtypeStruct(st["sc"].shape, F32),
            jax.ShapeDtypeStruct(st["bc"].shape, F32),
            jax.ShapeDtypeStruct(st["hs"].shape, F32),
            jax.ShapeDtypeStruct(st["cf"].shape, F32)]
    return _call(
        _spre_kernel,
        grid=(1,),
        in_specs=[_full(a.shape) for a in ins],
        out_specs=[_full(o.shape) for o in outs],
        out_shape=outs,
        scratch_shapes=[pltpu.VMEM((nb, SSD_CD), F32), pltpu.VMEM((LANES, WG), F32),
                        pltpu.VMEM((LANES, WG), F32)],
        compiler_params=_cparams(("arbitrary",), VMEM_BIG),
        name="sample_pre",
    )(*ins)


def _scmp_kernel(pt_ref, *refs, pg, n_groups, layer):
    del pt_ref, layer
    pages = refs[:pg]
    (perm_ref, wbd_ref, pe_ref, b1_ref, w2_ref, b2_ref, kg_ref, q_ref, imp_ref,
     oc_ref, psel_ref, permscr, pre) = refs[pg:]
    g = pl.program_id(1)
    cpp = PAGE // CMP_STRIDE
    for p in range(pg):
        both = pages[p][0, 0].reshape(2 * HD, PAGE)
        kp = _bdot_nt(perm_ref[...], both)
        for j in range(CMP_STRIDE):
            permscr[j, cpp * p:cpp * (p + 1), :] = kp[cpp * j:cpp * (j + 1), :]
    acc = None
    for j in range(CMP_STRIDE):
        term = _bdot(permscr[j], wbd_ref[j])
        acc = term if acc is None else acc + term
    rows = cpp * pg
    pre[pl.ds(pl.multiple_of(g * rows, rows), rows), :] = acc

    @pl.when(g == n_groups - 1)
    def _():
        nc = rows * n_groups
        pre[nc:nc + 8, :] = jnp.zeros((8, pre.shape[1]), F32)
        kc, vc = _cmp_tail(pre, nc, pe_ref, b1_ref, w2_ref, b2_ref, kg_ref[0:1, :])
        s = _bdot_nt(q_ref[0], kc)
        c_io = lax.broadcasted_iota(I32, (8, nc), 1)
        valid = c_io <= nc - 2
        sm = jnp.where(valid, s, NEG)
        m = jnp.max(sm, axis=1, keepdims=True)
        e = jnp.where(valid, jnp.exp2(sm - m), 0.0)
        p = e / jnp.sum(e, axis=1, keepdims=True)
        oc_ref[0] = _bdot(p, vc)
        ps8 = _dot_exact_rhs(p, imp_ref[...])
        psel_ref[0] = ps8[0:1, :] + ps8[1:2, :] + ps8[2:3, :] + ps8[3:4, :]


def _scmp(ckv_t, page_table, layer, q8, wts, consts):
    nb, n_pages = page_table.shape
    pg = min(16, n_pages)
    n_groups = n_pages // pg
    nc = n_pages * (PAGE // CMP_STRIDE)
    hid4 = 4 * wts["cmp_b1"].shape[-1]
    nsp = consts["imp"].shape[1]

    def page_spec(k):
        return pl.BlockSpec((1, 1, 2, HD, PAGE), lambda n, g, pt: (layer, pt[n, g * pg + k], 0, 0, 0))

    ws = [consts["perm"], wts["wbd"], wts["pe"], wts["cmp_b1"], wts["cmp_w2"], wts["cmp_b2"], wts["k_g"]]
    in_specs = ([page_spec(k) for k in range(pg)]
                + [pl.BlockSpec(w.shape, functools.partial(lambda nd, n, g, pt: (0,) * nd, w.ndim)) for w in ws]
                + [pl.BlockSpec((1, 8, HD), lambda n, g, pt: (n, 0, 0)),
                   pl.BlockSpec(consts["imp"].shape, lambda n, g, pt: (0, 0))])
    kern = functools.partial(_scmp_kernel, pg=pg, n_groups=n_groups, layer=layer)
    return pl.pallas_call(
        kern,
        grid_spec=pltpu.PrefetchScalarGridSpec(
            num_scalar_prefetch=1,
            grid=(nb, n_groups),
            in_specs=in_specs,
            out_specs=[pl.BlockSpec((1, 8, HD), lambda n, g, pt: (n, 0, 0)),
                       pl.BlockSpec((1, 1, nsp), lambda n, g, pt: (n, 0, 0))],
            scratch_shapes=[pltpu.VMEM((CMP_STRIDE, pg * PAGE // CMP_STRIDE, 2 * HD), F32),
                            pltpu.VMEM((nc + 8, hid4), F32)],
        ),
        out_shape=[jax.ShapeDtypeStruct((nb, 8, HD), F32), jax.ShapeDtypeStruct((nb, 1, nsp), F32)],
        compiler_params=_cparams(("arbitrary", "arbitrary"), VMEM_BIG),
        name="sample_cmp",
    )(page_table, *([ckv_t] * pg), *ws, q8, consts["imp"])


def _stopk_kernel(p_ref, o_ref, *, cur, n_sel):
    p = p_ref[...]
    j_io = lax.broadcasted_iota(I32, p.shape, 1)
    j_f = j_io.astype(F32)
    validb = j_io <= cur
    forced = validb & ((j_io == 0) | (j_io > cur - 2))
    score = jnp.where(forced, jnp.inf, jnp.where(validb, p, -jnp.inf))
    selm = jnp.zeros(p.shape, F32)
    out = jnp.zeros(o_ref.shape, I32)
    o_io = lax.broadcasted_iota(I32, o_ref.shape, 1)
    for k in range(n_sel):
        free = selm == 0.0
        mx = jnp.max(jnp.where(free, score, -jnp.inf), axis=1, keepdims=True)
        hit = free & (score == mx)
        first = jnp.min(jnp.where(hit, j_f, float(p.shape[1])), axis=1, keepdims=True)
        selm = jnp.where(j_f == first, 1.0, selm)
        out = jnp.where(o_io == k, first.astype(I32), out)
    o_ref[...] = out


def _stopk(psel, cur, n_sel):
    nb = psel.shape[0]
    return _call(
        functools.partial(_stopk_kernel, cur=cur, n_sel=n_sel),
        grid=(1,),
        in_specs=[_full(psel.shape)],
        out_specs=_full((nb, LANES)),
        out_shape=jax.ShapeDtypeStruct((nb, LANES), I32),
        name="sample_topk",
    )(psel)


def _flash_rows(carry, s, vt):
    m, l, acc = carry
    mn = jnp.maximum(m, jnp.max(s, axis=1, keepdims=True))
    alpha = jnp.exp2(m - mn)
    p = jnp.exp2(s - mn)
    l = alpha * l + jnp.sum(p, axis=1, keepdims=True)
    acc = alpha * acc + _bdot_nt(p, vt)
    return mn, l, acc


def _bf(x):
    return x.astype(BF16).astype(F32)


def _row_to_col(row):
    n = row.shape[1]
    diag = lax.broadcasted_iota(I32, (n, n), 0) == lax.broadcasted_iota(I32, (n, n), 1)
    return jnp.sum(jnp.where(diag, jnp.broadcast_to(row, (n, n)), 0.0), axis=1, keepdims=True)


def _ssel_kernel(idx_ref, pt_ref, *refs, n_sel, cur, layer):
    del pt_ref, layer
    pages = refs[:n_sel]
    cw_ref, q_ref, new_ref, g_ref, oc_ref, o_ref, wo_ref = refs[n_sel:]
    n = pl.program_id(0)
    q8 = q_ref[0]
    new = new_ref[0]
    lane_half = lax.broadcasted_iota(I32, (8, PAGE), 1) >> SEL_SHIFT

    carry = (jnp.full((8, 1), 2 * NEG, F32), jnp.zeros((8, 1), F32), jnp.zeros((8, HD), F32))
    for k in range(n_sel):
        blk = idx_ref[n, k]
        s = _bdot(q8, pages[k][0, 0, 0])
        want = jnp.where(blk < cur, blk & 1, 2)
        carry = _flash_rows(carry, jnp.where(lane_half == want, s, NEG), pages[k][0, 0, 1])
    m, l, acc = carry
    s_new = jnp.sum(_bf(q8) * _bf(new[0:1, :]), axis=1, keepdims=True)
    mn = jnp.maximum(m, s_new)
    alpha = jnp.exp2(m - mn)
    p_new = jnp.exp2(s_new - mn)
    o_s = (alpha * acc + _bf(p_new) * _bf(new[1:2, :])) / (alpha * l + p_new)

    kw = cw_ref[0, 0, 0]
    vw = cw_ref[0, 0, 1]
    s_w = _bdot(q8, kw)
    s_wn = jnp.sum(_bf(q8) * _bf(new[2:3, :]), axis=1, keepdims=True)
    mw = jnp.maximum(jnp.max(s_w, axis=1, keepdims=True), s_wn)
    e = jnp.exp2(s_w - mw)
    e_n = jnp.exp2(s_wn - mw)
    o_w = (_bdot_nt(e, vw) + _bf(e_n) * _bf(new[3:4, :])) / (jnp.sum(e, axis=1, keepdims=True) + e_n)

    g = g_ref[0]
    o_ref[0] = g[:, 0:1] * oc_ref[0] + g[:, 1:2] * o_s + g[:, 2:3] * o_w

    wb = kw.shape[1]
    last = lax.broadcasted_iota(I32, (HD, wb), 1) == wb - 1
    wo_ref[0, 0, 0] = jnp.where(last, _row_to_col(new[2:3, :]), pltpu.roll(kw, wb - 1, axis=1))
    wo_ref[0, 0, 1] = jnp.where(last, _row_to_col(new[3:4, :]), pltpu.roll(vw, wb - 1, axis=1))


def _ssel(ckv_t, cw_t, idx, page_table, layer, q8, new8, g8, oc):
    nb, n_pages = page_table.shape
    n_sel = idx.shape[1]
    wb = cw_t.shape[-1]
    cur = n_pages * (PAGE // SEL_BLOCK)

    def page_spec(k):
        def imap(n, ix, pt):
            return (layer, pt[n, jnp.minimum(ix[n, k] // 2, n_pages - 1)], 1, 0, 0)
        return pl.BlockSpec((1, 1, 2, HD, PAGE), imap)

    in_specs = ([page_spec(k) for k in range(n_sel)]
                + [pl.BlockSpec((1, 1, 2, HD, wb), lambda n, ix, pt: (layer, n, 0, 0, 0)),
                   pl.BlockSpec((1, 8, HD), lambda n, ix, pt: (n, 0, 0)),
                   pl.BlockSpec((1, 8, HD), lambda n, ix, pt: (n, 0, 0)),
                   pl.BlockSpec((1, 8, 3), lambda n, ix, pt: (n, 0, 0)),
                   pl.BlockSpec((1, 8, HD), lambda n, ix, pt: (n, 0, 0))])
    kern = functools.partial(_ssel_kernel, n_sel=n_sel, cur=cur, layer=layer)
    return pl.pallas_call(
        kern,
        grid_spec=pltpu.PrefetchScalarGridSpec(
            num_scalar_prefetch=2,
            grid=(nb,),
            in_specs=in_specs,
            out_specs=[pl.BlockSpec((1, 8, HD), lambda n, ix, pt: (n, 0, 0)),
                       pl.BlockSpec((1, 1, 2, HD, wb), lambda n, ix, pt: (0, n, 0, 0, 0))],
        ),
        out_shape=[jax.ShapeDtypeStruct((nb, 8, HD), F32),
                   jax.ShapeDtypeStruct((1, nb, 2, HD, wb), F32)],
        compiler_params=_cparams(("arbitrary",)),
        name="sample_sel",
    )(idx, page_table, *([ckv_t] * n_sel), cw_t, q8, new8, g8, oc)


def _spost_kernel(x_ref, yabc_ref, sgd_ref, o_ref, ada_ref, w_ref, out_ref):
    y = jnp.concatenate([yabc_ref[...], sgd_ref[...] * o_ref[...]], axis=1)
    out_ref[...] = x_ref[...] + ada_ref[:, 2 * D_MODEL:3 * D_MODEL] * _bdot(y, w_ref[...])


def _spost(x, yabc, sgd, o, ada_rows, w):
    ins = [x, yabc, sgd, o, ada_rows, w]
    return _call(
        _spost_kernel,
        grid=(1,),
        in_specs=[_full(a.shape) for a in ins],
        out_specs=_full(x.shape),
        out_shape=jax.ShapeDtypeStruct(x.shape, F32),
        name="sample_post",
    )(*ins)


def _pad_cols(w, n):
    return jnp.concatenate([w, jnp.zeros(w.shape[:-1] + (n - w.shape[-1],), w.dtype)], axis=-1)


def _prep_w_in(w):
    o = np.cumsum([0, 256, 256, 256, 256, 256, 768, 4, 512, 256, 256, 384, 256, 12]).tolist()
    sec = [w[:, o[k]:o[k + 1]] for k in range(13)]
    a_h, a_b, a_c, a_g, b_z, b_xbc, b_dt, c_glu, c_g, d_q, d_kv, d_g, d_bg = sec
    cols = [a_h, a_b, a_c, a_g, b_z, b_xbc, _pad_cols(b_dt, LANES), c_glu, c_g,
            d_q, d_g, d_kv, _pad_cols(d_bg, LANES)]
    return jnp.concatenate(cols, axis=1).astype(BF16)


def _prep_wbd(w1):
    hid = w1.shape[-1]
    z = jnp.zeros((CMP_STRIDE, HD, hid), w1.dtype)
    top = jnp.concatenate([w1[0, :CMP_STRIDE], w1[0, CMP_STRIDE:], z, z], axis=-1)
    bot = jnp.concatenate([z, z, w1[1, :CMP_STRIDE], w1[1, CMP_STRIDE:]], axis=-1)
    return jnp.concatenate([top, bot], axis=1).astype(BF16)


def _layer_weights(l, a, pe_all):
    row = lambda v: v.reshape(1, -1)
    return {
        "sc_w": a["sc_w"][l], "bc_w": a["ssd_conv_w"][l], "bc_b": row(a["ssd_conv_b"][l]),
        "dtb128": _pad_cols(row(a["ssd_dt_bias"][l]), LANES),
        "alog128": _pad_cols(row(a["ssd_A_log"][l]), LANES),
        "alog256": row(jnp.repeat(a["ssd_A_log"][l], SSD_P)),
        "dvec": row(jnp.repeat(a["ssd_D"][l], SSD_P)),
        "ssd_ng": row(a["ssd_norm_g"][l]),
        "cf_w": a["cf_w"][l], "cf_b": row(a["cf_b"][l]),
        "ln_g": row(a["cf_ln_g"][l]), "ln_b": row(a["cf_ln_b"][l]),
        "q_g": row(jnp.tile(a["q_g"][l], NH)), "k_g": a["k_g"][l],
        "wbd": _prep_wbd(a["cmp_w1"][l]), "pe": pe_all[2 * l:2 * l + 2],
        "cmp_b1": a["cmp_b1"][l].reshape(2, 1, -1), "cmp_w2": a["cmp_w2"][l],
        "cmp_b2": a["cmp_b2"][l].reshape(2, 1, -1),
    }


def _sample_consts(n_pages):
    cpp = PAGE // CMP_STRIDE
    perm = np.zeros((PAGE, PAGE), np.float32)
    for j in range(CMP_STRIDE):
        for m in range(cpp):
            perm[cpp * j + m, CMP_STRIDE * m + j] = 1.0
    nc = n_pages * cpp
    ns = n_pages * (PAGE // SEL_BLOCK) + 1
    nsp = -(-ns // LANES) * LANES
    imp = np.zeros((nc, nsp), np.float32)
    r = SEL_BLOCK // CMP_STRIDE
    for j in range(ns):
        for o, wgt in ((-1, 0.5), (0, 1.0), (1, 1.0), (2, 1.0), (3, 0.5)):
            c = r * j + o
            if 0 <= c < nc:
                imp[c, j] = wgt
    return {"perm": jnp.asarray(perm, BF16), "imp": jnp.asarray(imp, BF16)}


def kernel(x_prompt, x_sample, cache_kv, cache_win, state_sconv, state_ssm_conv, state_ssm, state_cconv, page_table, c_prompt, c_sample, norm_g, w_ada, b_ada, w_in, w_out, sc_w, ssd_conv_w, ssd_conv_b, ssd_dt_bias, ssd_A_log, ssd_D, ssd_norm_g, cf_w, cf_b, cf_ln_g, cf_ln_b, q_g, k_g, cmp_pe, cmp_w1, cmp_b1, cmp_w2, cmp_b2):
    a = dict(sc_w=sc_w, ssd_conv_w=ssd_conv_w, ssd_conv_b=ssd_conv_b, ssd_dt_bias=ssd_dt_bias,
             ssd_A_log=ssd_A_log, ssd_D=ssd_D, ssd_norm_g=ssd_norm_g, cf_w=cf_w, cf_b=cf_b,
             cf_ln_g=cf_ln_g, cf_ln_b=cf_ln_b, q_g=q_g, k_g=k_g, cmp_w1=cmp_w1, cmp_b1=cmp_b1,
             cmp_w2=cmp_w2, cmp_b2=cmp_b2)
    depth = w_in.shape[0]
    assert x_prompt.shape[0] == 1 and x_sample.shape[1] == 1
    s = x_prompt.shape[1]
    nb = x_sample.shape[0]
    n_pages = page_table.shape[1]
    past = n_pages * PAGE
    assert s % (AUG // 2 * SEL_BLOCK) == 0 and s >= WINDOW + LANES and cache_win.shape[2] == min(WINDOW, past)

    n_c = 1 + nb
    c_all = jnp.concatenate([c_prompt, c_sample, jnp.zeros((-n_c % 8, D_MODEL), F32)], axis=0)
    ada = _ada(c_all, w_ada, b_ada)
    pe_all = _pe_terms(cmp_pe, cmp_w1)

    ckv_t = jnp.transpose(cache_kv, (0, 1, 3, 4, 2))
    cw_t = jnp.transpose(cache_win, (0, 1, 3, 4, 2))
    consts = _sample_consts(n_pages)

    xp = x_prompt[0]
    xs = x_sample[:, 0]
    outs_p, outs_s = [], []
    for l in range(depth):
        wts = _layer_weights(l, a, pe_all)
        w_in_l = _prep_w_in(w_in[l])
        w_out_l = w_out[l].astype(BF16)
        g_l = norm_g[l].reshape(1, -1)

        ada_p = ada[l, 0:1]
        pabc, pd = _inproj(xp, ada_p, g_l, w_in_l)
        yabc, sc_p, bc_p, hs_p, cf_p = _mix(pabc, wts)
        qT, ks, kw, vT, kvp, winT, gT, kmax = _nsaprep(pd, wts["q_g"], wts["k_g"])
        kc, vcT = _pcompress(pd, wts)
        yd = _attn(qT, ks, kw, vT, kc, vcT, gT, pd, kmax)
        xp = _outproj(xp, yabc, yd, ada_p, w_out_l)
        wlen = min(WINDOW, s)
        outs_p.append((jnp.transpose(kvp, (0, 3, 1, 2)),
                       jnp.transpose(winT[:, :, s - wlen:], (2, 0, 1))[None],
                       sc_p[None], bc_p[None], hs_p.reshape(1, SSD_H, SSD_P, SSD_S), cf_p[None]))

        ada_s = ada[l, 1:1 + nb]
        st = {"sc": jnp.transpose(state_sconv[l], (1, 0, 2)), "bc": jnp.transpose(state_ssm_conv[l], (1, 0, 2)),
              "hs": state_ssm[l].reshape(nb, SSD_H * SSD_P, SSD_S), "cf": jnp.transpose(state_cconv[l], (1, 0, 2))}
        (yabc_s, q_s, rows_s, win_s, gates_s, sgd_s, sc_s, bc_s, hs_s, cf_s) = _spre(xs, ada_s, g_l, w_in_l, st, wts)
        pad8 = lambda v: jnp.concatenate([v, jnp.zeros_like(v)], axis=1)
        q8 = pad8(q_s.reshape(nb, NH, HD))
        oc, psel = _scmp(ckv_t, page_table, l, q8, wts, consts)
        ns = past // SEL_BLOCK + 1
        idx = _stopk(psel.reshape(nb, -1), ns - 1, min(SEL_TOPN, ns))[:, :min(SEL_TOPN, ns)]
        new8 = pad8(jnp.concatenate([rows_s[:, 2 * HD:4 * HD], win_s], axis=1).reshape(nb, 4, HD))
        g8 = pad8(gates_s[:, 0:3 * NH].reshape(nb, NH, 3))
        o_s, win_new = _ssel(ckv_t, cw_t, idx, page_table, l, q8, new8, g8, oc)
        xs = _spost(xs, yabc_s, sgd_s, o_s[:, 0:NH].reshape(nb, WG), ada_s, w_out_l)
        outs_s.append((rows_s.reshape(nb, 1, 4, HD),
                       jnp.transpose(win_new[0], (0, 3, 1, 2)),
                       jnp.transpose(sc_s, (1, 0, 2)), jnp.transpose(bc_s, (1, 0, 2)),
                       hs_s.reshape(nb, SSD_H, SSD_P, SSD_S), jnp.transpose(cf_s, (1, 0, 2))))

    kv_p, win_p, sc_p, bc_p, h_p, cf_p = [jnp.stack(v) for v in zip(*outs_p)]
    kv_s, win_s, sc_s, bc_s, h_s, cf_s = [jnp.stack(v) for v in zip(*outs_s)]
    return (xp[None], xs[:, None], kv_p, kv_s, win_p, win_s, sc_p, sc_s, bc_p, bc_s, h_p, h_s, cf_p, cf_s)
```

```python
import functools

import numpy as np
import jax
import jax.numpy as jnp
from jax import lax
from jax.experimental import pallas as pl
from jax.experimental.pallas import tpu as pltpu

F32 = jnp.float32
BF16 = jnp.bfloat16
I32 = jnp.int32

D_MODEL = 1024
WG = 256
SC_K = 3
SSD_K = 4
SSD_CD = 768
SSD_H = 4
SSD_P = 64
SSD_S = 128
CF_K = 31
HD = 64
NH = 4
CMP_STRIDE = 16
SEL_BLOCK = 64
SEL_SHIFT = 6
SEL_TOPN = 16
SEL_LOCAL = 2
WINDOW = 512
PAGE = 128
EPS = 1e-6
NEG = -1e30
Q_SCALE = HD ** -0.5 * float(np.log2(np.e))
SAFE_EXP2 = 60.0
AUG = 64
HALO = 32
LANES = 128

N_ABC = 2944
N_D = 1024
N_IN = N_ABC + N_D
O_BZ, O_BX, O_BDT, O_CGLU, O_CG = 1024, 1280, 2048, 2176, 2688
O_DQ, O_DG, O_DKV, O_DBG = 0, 256, 512, 896

VMEM_BIG = 56 * 1024 * 1024


def _silu(x):
    return x * jax.nn.sigmoid(x)


def _softplus(x):
    return jnp.maximum(x, 0.0) + jnp.log1p(jnp.exp(-jnp.abs(x)))


def _bdot(a, b):
    return jnp.dot(a.astype(BF16), b.astype(BF16), preferred_element_type=F32)


def _bdot_nt(a, b):
    return lax.dot_general(a.astype(BF16), b.astype(BF16), (((1,), (1,)), ((), ())),
                           preferred_element_type=F32)


def _split3(a):
    a1 = a.astype(BF16)
    r1 = a - a1.astype(F32)
    a2 = r1.astype(BF16)
    a3 = (r1 - a2.astype(F32)).astype(BF16)
    return a1, a2, a3


def _dot_exact_rhs(a, m_bf16):
    a1, a2, a3 = _split3(a)
    d = functools.partial(jnp.dot, preferred_element_type=F32)
    return d(a1, m_bf16) + d(a2, m_bf16) + d(a3, m_bf16)


def _dot_exact_lhs(m_bf16, a):
    a1, a2, a3 = _split3(a)
    d = functools.partial(jnp.dot, preferred_element_type=F32)
    return d(m_bf16, a1) + d(m_bf16, a2) + d(m_bf16, a3)


def _group_ones(n, group):
    sh = group.bit_length() - 1
    assert group == 1 << sh
    r = lax.broadcasted_iota(I32, (n, n), 0) >> sh
    c = lax.broadcasted_iota(I32, (n, n), 1) >> sh
    return jnp.where(r == c, 1.0, 0.0).astype(BF16)


def _group_rms(x, group):
    ssq = _dot_exact_rhs(x * x, _group_ones(x.shape[-1], group))
    return x * lax.rsqrt(ssq * (1.0 / group) + EPS)


def _rms(x):
    return x * lax.rsqrt(jnp.mean(x * x, axis=-1, keepdims=True) + EPS)


def _modulated_norm(x, g, scale, shift):
    return _rms(x) * g * (1.0 + scale) + shift


def _call(kernel, **kw):
    return pl.pallas_call(kernel, **kw)


def _cparams(sem=None, vmem=None):
    kw = {}
    if sem is not None:
        kw["dimension_semantics"] = sem
    if vmem is not None:
        kw["vmem_limit_bytes"] = vmem
    return pltpu.CompilerParams(**kw)


def _full(shape):
    nd = len(shape)
    return pl.BlockSpec(shape, lambda *_: (0,) * nd)


def _ada_kernel(c_ref, w_ref, b_ref, o_ref):
    o_ref[0] = _bdot(_silu(c_ref[...]), w_ref[0]) + b_ref[0]


def _ada(c_all, w_ada, b_ada):
    depth = w_ada.shape[0]
    rows = c_all.shape[0]
    tn = 512
    return _call(
        _ada_kernel,
        grid=(depth, 3 * D_MODEL // tn),
        in_specs=[pl.BlockSpec((rows, D_MODEL), lambda l, j: (0, 0)),
                  pl.BlockSpec((1, D_MODEL, tn), lambda l, j: (l, 0, j)),
                  pl.BlockSpec((1, 1, tn), lambda l, j: (l, 0, j))],
        out_specs=pl.BlockSpec((1, rows, tn), lambda l, j: (l, 0, j)),
        out_shape=jax.ShapeDtypeStruct((depth, rows, 3 * D_MODEL), F32),
        name="ada",
    )(c_all, w_ada, b_ada.reshape(depth, 1, 3 * D_MODEL))


def _pe_kernel(pe_ref, w_ref, o_ref):
    o_ref[0] = jnp.sum(pe_ref[0] * w_ref[0], axis=0, keepdims=True)


def _pe_terms(cmp_pe, cmp_w1):
    depth = cmp_pe.shape[0]
    n = depth * 2
    kk = cmp_pe.shape[2] * cmp_pe.shape[3]
    hid = cmp_w1.shape[-1]
    return _call(
        _pe_kernel,
        grid=(n,),
        in_specs=[pl.BlockSpec((1, kk, 1), lambda i: (i, 0, 0)),
                  pl.BlockSpec((1, kk, hid), lambda i: (i, 0, 0))],
        out_specs=pl.BlockSpec((1, 1, hid), lambda i: (i, 0, 0)),
        out_shape=jax.ShapeDtypeStruct((n, 1, hid), F32),
        name="pe_term",
    )(cmp_pe.reshape(n, kk, 1), cmp_w1.reshape(n, kk, hid))


def _inproj_kernel(x_ref, ada_ref, g_ref, w_ref, oabc_ref, od_ref):
    shift = ada_ref[:, 0:D_MODEL]
    scale = ada_ref[:, D_MODEL:2 * D_MODEL]
    h = _modulated_norm(x_ref[...], g_ref[...], scale, shift)
    p = _bdot(h, w_ref[...])
    oabc_ref[...] = p[:, :N_ABC]
    od_ref[...] = p[:, N_ABC:]


def _inproj(x, ada_row, g, w):
    s = x.shape[0]
    tr = 256
    return _call(
        _inproj_kernel,
        grid=(s // tr,),
        in_specs=[pl.BlockSpec((tr, D_MODEL), lambda i: (i, 0)),
                  _full((1, 3 * D_MODEL)), _full((1, D_MODEL)), _full((D_MODEL, N_IN))],
        out_specs=[pl.BlockSpec((tr, N_ABC), lambda i: (i, 0)),
                   pl.BlockSpec((tr, N_D), lambda i: (i, 0))],
        out_shape=[jax.ShapeDtypeStruct((s, N_ABC), F32), jax.ShapeDtypeStruct((s, N_D), F32)],
        compiler_params=_cparams(("arbitrary",), VMEM_BIG),
        name="inproj",
    )(x, ada_row, g, w)


def _tile_conv(ext_ref, u, w_ref, taps, t):
    ext_ref[HALO:HALO + t, :] = u
    acc = None
    for k in range(taps):
        o = HALO - (taps - 1) + k
        term = w_ref[k:k + 1, :] * ext_ref[o:o + t, :]
        acc = term if acc is None else acc + term
    return acc


def _ssd_chunk(xs, bm, cm, dtp, a, hst_ref):
    t = xs.shape[0]
    ri = lax.broadcasted_iota(I32, (t, t), 0)
    ci = lax.broadcasted_iota(I32, (t, t), 1)
    tril = ri >= ci
    trilb = jnp.where(tril, 1.0, 0.0).astype(BF16)
    triub = jnp.where(ri <= ci, 1.0, 0.0).astype(BF16)
    acs_col = _dot_exact_lhs(trilb, a)
    acs_row = _dot_exact_rhs(a.T, triub)
    ys = []
    for g in range(2):
        bg = bm[:, SSD_S * g:SSD_S * (g + 1)]
        cg = cm[:, SSD_S * g:SSD_S * (g + 1)]
        gram = _bdot_nt(cg, bg)
        xte = []
        for hh in range(2):
            h = 2 * g + hh
            ac = acs_col[:, h:h + 1]
            ar = acs_row[h:h + 1, :]
            a_last = acs_col[t - 1:t, h:h + 1]
            decay = jnp.where(tril, jnp.exp(jnp.where(tril, ac - ar, 0.0)), 0.0)
            xdt = xs[:, SSD_P * h:SSD_P * (h + 1)] * dtp[:, h:h + 1]
            y_diag = _bdot(gram * decay, xdt)
            hprev = hst_ref[SSD_P * h:SSD_P * (h + 1), :]
            y_off = _bdot_nt(cg, hprev) * jnp.exp(ac)
            ys.append(y_diag + y_off)
            xte.append(xdt * jnp.exp(a_last - ac))
        x2t = jnp.concatenate(xte, axis=1).T
        states = _bdot(x2t, bg)
        for hh in range(2):
            h = 2 * g + hh
            cd = jnp.exp(acs_col[t - 1:t, h:h + 1])
            sl = slice(SSD_P * h, SSD_P * (h + 1))
            hst_ref[sl, :] = cd * hst_ref[sl, :] + states[SSD_P * hh:SSD_P * (hh + 1), :]
    return jnp.concatenate(ys, axis=1)


def _mix_kernel(p_ref, scw_ref, bcw_ref, bcb_ref, dtb_ref, alog_ref, dvec_ref, ng_ref,
                cfw_ref, cfb_ref, lng_ref, lnb_ref,
                y_ref, sc_ref, bc_ref, hs_ref, cf_ref,
                exta, extb, extc, hst):
    t = p_ref.shape[0]
    i = pl.program_id(0)

    @pl.when(i == 0)
    def _():
        exta[0:HALO, :] = jnp.zeros((HALO, WG), F32)
        extb[0:HALO, :] = jnp.zeros((HALO, SSD_CD), F32)
        extc[0:HALO, :] = jnp.zeros((HALO, WG), F32)
        hst[...] = jnp.zeros_like(hst)

    u_a = p_ref[:, 512:768] * p_ref[:, 0:256]
    conv_a = _tile_conv(exta, u_a, scw_ref, SC_K, t)
    y_a = _silu(p_ref[:, 768:1024]) * p_ref[:, 256:512] * conv_a

    conv_b = _tile_conv(extb, p_ref[:, O_BX:O_BX + SSD_CD], bcw_ref, SSD_K, t) + bcb_ref[...]
    xbc = _silu(conv_b)
    xs = xbc[:, 0:WG]
    dtp = _softplus(p_ref[:, O_BDT:O_BDT + LANES] + dtb_ref[...])
    a = dtp * (-jnp.exp(alog_ref[...]))
    y_ssd = _ssd_chunk(xs, xbc[:, WG:WG + 2 * SSD_S], xbc[:, WG + 2 * SSD_S:], dtp, a, hst)
    y_b = y_ssd + dvec_ref[...] * xs
    y_b = _rms(y_b * _silu(p_ref[:, O_BZ:O_BZ + WG])) * ng_ref[...]

    glu = p_ref[:, O_CGLU:O_CGLU + WG] * jax.nn.sigmoid(p_ref[:, O_CGLU + WG:O_CGLU + 2 * WG])
    u_c = _tile_conv(extc, glu, cfw_ref, CF_K, t) + cfb_ref[...]
    mu = jnp.mean(u_c, axis=-1, keepdims=True)
    var = jnp.mean(jnp.square(u_c - mu), axis=-1, keepdims=True)
    ln = (u_c - mu) * lax.rsqrt(var + EPS) * lng_ref[...] + lnb_ref[...]
    y_c = _silu(p_ref[:, O_CG:O_CG + WG]) * _silu(ln)

    y_ref[:, 0:WG] = y_a
    y_ref[:, WG:2 * WG] = y_b
    y_ref[:, 2 * WG:3 * WG] = y_c

    sc_ref[...] = exta[HALO + t - (SC_K - 1):HALO + t, :]
    bc_ref[...] = extb[HALO + t - (SSD_K - 1):HALO + t, :]
    cf_ref[...] = extc[HALO + t - (CF_K - 1):HALO + t, :]
    hs_ref[...] = hst[...]
    exta[0:HALO, :] = exta[t:t + HALO, :]
    extb[0:HALO, :] = extb[t:t + HALO, :]
    extc[0:HALO, :] = extc[t:t + HALO, :]


def _mix(pabc, wts):
    s = pabc.shape[0]
    t = SSD_S
    small = [wts[k] for k in ("sc_w", "bc_w", "bc_b", "dtb128", "alog128", "dvec", "ssd_ng",
                              "cf_w", "cf_b", "ln_g", "ln_b")]
    return _call(
        _mix_kernel,
        grid=(s // t,),
        in_specs=[pl.BlockSpec((t, N_ABC), lambda i: (i, 0))] + [_full(w.shape) for w in small],
        out_specs=[pl.BlockSpec((t, 3 * WG), lambda i: (i, 0)),
                   _full((SC_K - 1, WG)), _full((SSD_K - 1, SSD_CD)),
                   _full((SSD_H * SSD_P, SSD_S)), _full((CF_K - 1, WG))],
        out_shape=[jax.ShapeDtypeStruct((s, 3 * WG), F32),
                   jax.ShapeDtypeStruct((SC_K - 1, WG), F32),
                   jax.ShapeDtypeStruct((SSD_K - 1, SSD_CD), F32),
                   jax.ShapeDtypeStruct((SSD_H * SSD_P, SSD_S), F32),
                   jax.ShapeDtypeStruct((CF_K - 1, WG), F32)],
        scratch_shapes=[pltpu.VMEM((HALO + t, WG), F32), pltpu.VMEM((HALO + t, SSD_CD), F32),
                        pltpu.VMEM((HALO + t, WG), F32), pltpu.VMEM((SSD_H * SSD_P, SSD_S), F32)],
        compiler_params=_cparams(("arbitrary",)),
        name="mix",
    )(pabc, *small)


def _nsa_rows(pd, qg, kg1, kg2):
    q = _group_rms(pd[:, O_DQ:O_DQ + WG], HD) * qg * Q_SCALE
    kv = pd[:, O_DKV:O_DKV + 6 * HD]
    k_slc = _rms(kv[:, 2 * HD:3 * HD]) * kg1
    k_win = _rms(kv[:, 4 * HD:5 * HD]) * kg2
    rows = jnp.concatenate([kv[:, 0:2 * HD], k_slc, kv[:, 3 * HD:4 * HD]], axis=1)
    win = jnp.concatenate([k_win, kv[:, 5 * HD:6 * HD]], axis=1)
    gates = jax.nn.sigmoid(pd[:, O_DBG:O_DBG + LANES])
    return q, rows, win, gates


def _row_norm_max(k):
    kb = k.astype(BF16).astype(F32)
    return jnp.sqrt(jnp.max(jnp.sum(kb * kb, axis=1, keepdims=True), axis=0, keepdims=True))


def _nsaprep_kernel(pd_ref, qg_ref, kg_ref, qT_ref, ks_ref, kw_ref, vT_ref, kvp_ref, winT_ref, gT_ref, kmax_ref):
    tr = pd_ref.shape[0]
    i = pl.program_id(0)
    q, rows, win, gates = _nsa_rows(pd_ref[...], qg_ref[...], kg_ref[1:2, :], kg_ref[2:3, :])
    qT_ref[...] = q.T.astype(BF16)
    k_slc = rows[:, 2 * HD:3 * HD]
    k_win = win[:, 0:HD]
    col = lax.broadcasted_iota(I32, (tr, AUG), 1)
    blk = ((i * tr + lax.broadcasted_iota(I32, (tr, AUG), 0)) >> SEL_SHIFT) & (AUG // 2 - 1)
    ones_col = col == AUG // 2
    ks_ref[...] = jnp.concatenate([k_slc, jnp.where((col == blk) | ones_col, 1.0, 0.0)], axis=1).astype(BF16)
    kw_ref[...] = jnp.concatenate([k_win, jnp.where(ones_col, 1.0, 0.0)], axis=1).astype(BF16)
    kmax = jnp.broadcast_to(_row_norm_max(k_slc), (8, LANES))

    @pl.when(i == 0)
    def _():
        kmax_ref[...] = kmax

    @pl.when(i > 0)
    def _():
        kmax_ref[...] = jnp.maximum(kmax_ref[...], kmax)

    rows_t = rows.T
    win_t = win.T
    vT_ref[...] = jnp.concatenate([rows_t[3 * HD:4 * HD, :], win_t[HD:2 * HD, :]], axis=0).astype(BF16)
    for p in range(tr // PAGE):
        kvp_ref[p] = rows_t[:, PAGE * p:PAGE * (p + 1)].reshape(4, HD, PAGE)
    winT_ref[...] = win_t.reshape(2, HD, tr)
    gT_ref[...] = gates.T[0:16, :]


def _nsaprep(pd, qg, kg):
    s = pd.shape[0]
    tr = 256
    return _call(
        _nsaprep_kernel,
        grid=(s // tr,),
        in_specs=[pl.BlockSpec((tr, N_D), lambda i: (i, 0)), _full(qg.shape), _full(kg.shape)],
        out_specs=[pl.BlockSpec((WG, tr), lambda i: (0, i)),
                   pl.BlockSpec((tr, 2 * HD), lambda i: (i, 0)),
                   pl.BlockSpec((tr, 2 * HD), lambda i: (i, 0)),
                   pl.BlockSpec((2 * HD, tr), lambda i: (0, i)),
                   pl.BlockSpec((tr // PAGE, 4, HD, PAGE), lambda i: (i, 0, 0, 0)),
                   pl.BlockSpec((2, HD, tr), lambda i: (0, 0, i)),
                   pl.BlockSpec((16, tr), lambda i: (0, i)),
                   _full((8, LANES))],
        out_shape=[jax.ShapeDtypeStruct((WG, s), BF16),
                   jax.ShapeDtypeStruct((s, 2 * HD), BF16),
                   jax.ShapeDtypeStruct((s, 2 * HD), BF16),
                   jax.ShapeDtypeStruct((2 * HD, s), BF16),
                   jax.ShapeDtypeStruct((s // PAGE, 4, HD, PAGE), F32),
                   jax.ShapeDtypeStruct((2, HD, s), F32),
                   jax.ShapeDtypeStruct((16, s), F32),
                   jax.ShapeDtypeStruct((8, LANES), F32)],
        compiler_params=_cparams(("arbitrary",)),
        name="nsaprep",
    )(pd, qg, kg)


def _cmp_tail(pre_ref, nc, pe_ref, b1_ref, w2_ref, b2_ref, kg0):
    hid = b1_ref.shape[-1]
    outs = []
    for s in range(2):
        first = pre_ref[0:nc, 2 * s * hid:(2 * s + 1) * hid]
        second = pre_ref[pl.ds(1, nc), (2 * s + 1) * hid:(2 * s + 2) * hid]
        h = _silu(first + second + pe_ref[s] + b1_ref[s])
        outs.append(_bdot(h, w2_ref[s]) + b2_ref[s])
    return _rms(outs[0]) * kg0, outs[1]


def _pcompress_kernel(kv_ref, wbd_ref, pe_ref, b1_ref, w2_ref, b2_ref, kg_ref, kc_ref, vcT_ref, pre):
    nc = kv_ref.shape[0] // CMP_STRIDE
    acc = None
    for j in range(CMP_STRIDE):
        term = _bdot(kv_ref[pl.ds(j, nc, stride=CMP_STRIDE), :], wbd_ref[j])
        acc = term if acc is None else acc + term
    pre[0:nc, :] = acc
    pre[nc:nc + 8, :] = jnp.zeros((8, pre.shape[1]), F32)
    kc, vc = _cmp_tail(pre, nc, pe_ref, b1_ref, w2_ref, b2_ref, kg_ref[0:1, :])
    kc_ref[...] = kc.astype(BF16)
    vcT_ref[...] = jnp.concatenate([vc, jnp.zeros_like(vc)], axis=1).T[0:HD, :].astype(BF16)


def _pcompress(pd, wts):
    s = pd.shape[0]
    nc = s // CMP_STRIDE
    ws = [wts[k] for k in ("wbd", "pe", "cmp_b1", "cmp_w2", "cmp_b2", "k_g")]
    return _call(
        _pcompress_kernel,
        grid=(1,),
        in_specs=[pl.BlockSpec((s, 2 * HD), lambda i: (0, O_DKV // (2 * HD)))] + [_full(w.shape) for w in ws],
        out_specs=[_full((nc, HD)), _full((HD, nc))],
        out_shape=[jax.ShapeDtypeStruct((nc, HD), BF16), jax.ShapeDtypeStruct((HD, nc), BF16)],
        scratch_shapes=[pltpu.VMEM((nc + 8, 4 * wts["cmp_b1"].shape[-1]), F32)],
        compiler_params=_cparams(("arbitrary",), VMEM_BIG),
        name="pcompress",
    )(pd, *ws)


def _masked_softmax_cols(s, mask):
    sm = jnp.where(mask, s, NEG)
    m = jnp.max(sm, axis=0, keepdims=True)
    e = jnp.where(mask, jnp.exp2(sm - m), 0.0)
    l = jnp.sum(e, axis=0, keepdims=True)
    return e * jnp.where(l > 0.0, 1.0 / l, 0.0)


def _flash_cols(carry, s, vt):
    m, l, acc = carry
    mn = jnp.maximum(m, jnp.max(s, axis=0, keepdims=True))
    alpha = jnp.exp2(m - mn)
    p = jnp.exp2(s - mn)
    l = alpha * l + jnp.sum(p, axis=0, keepdims=True)
    acc = alpha * acc + jnp.dot(vt, p.astype(BF16), preferred_element_type=F32)
    return mn, l, acc


def _attn_kernel(qT_ref, ks_ref, kw_ref, vT_ref, kc_ref, vcT_ref, gT_ref, dg_ref, kmax_ref,
                 o_ref, pg_scr, bias_scr, *, n_sel, tk, n_sub):
    qb = LANES
    i = pl.program_id(0)
    finishers = []
    for sub in range(n_sub):
        lanes = slice(qb * sub, qb * (sub + 1))
        finishers.append(_attn_tile(
            (i * n_sub + sub) * qb, qT_ref[:, lanes], gT_ref[:, lanes], dg_ref[lanes, :],
            functools.partial(o_ref.__setitem__, (lanes, slice(None))),
            ks_ref, kw_ref, vT_ref, kc_ref, vcT_ref, kmax_ref, pg_scr.at[sub], bias_scr.at[sub],
            n_sel=n_sel, tk=tk))
    for finish in finishers:
        finish()


def _attn_tile(q0, qt, g_t, dg, write_o, ks_ref, kw_ref, vT_ref, kc_ref, vcT_ref, kmax_ref,
               pg_scr, bias_scr, *, n_sel, tk):
    qb = qt.shape[1]
    nq = NH * qb
    half = AUG // 2
    qall = jnp.concatenate([qt[HD * h:HD * (h + 1), :] for h in range(NH)], axis=1)

    def q_aug(extra):
        return jnp.concatenate([qall, extra.astype(BF16)], axis=0)

    qf = qall.astype(F32)
    qn = jnp.sqrt(jnp.sum(qf * qf, axis=0, keepdims=True))
    b_s = qn * kmax_ref[0:1, 0:1]
    fast = jnp.max(b_s) <= SAFE_EXP2
    row_a = lax.broadcasted_iota(I32, (half, nq), 0)
    zero_half = jnp.zeros((half, nq), F32)

    def stab_rows(b):
        return jnp.where(row_a == 0, -b, 0.0)

    nc = kc_ref.shape[0]
    s_c = jnp.dot(kc_ref[...], qall, preferred_element_type=F32)
    c_io = lax.broadcasted_iota(I32, (nc, nq), 0)
    qpos = q0 + (lax.broadcasted_iota(I32, (nc, nq), 1) & (qb - 1))
    p_c = _masked_softmax_cols(s_c, c_io * CMP_STRIDE + (2 * CMP_STRIDE - 1) <= qpos)
    o_c = jnp.dot(vcT_ref[...], p_c.astype(BF16), preferred_element_type=F32)

    wk = WINDOW + qb
    start = pl.multiple_of(jnp.maximum(q0 - WINDOW, 0), qb)
    s_w = jnp.dot(kw_ref[pl.ds(start, wk), :], q_aug(jnp.zeros((AUG, nq), F32)),
                  preferred_element_type=F32)
    dpos = (q0 + (lax.broadcasted_iota(I32, (wk, nq), 1) & (qb - 1))
            - (start + lax.broadcasted_iota(I32, (wk, nq), 0)))
    p_w = _masked_softmax_cols(s_w, (dpos >= 0) & (dpos <= WINDOW))
    o_w = jnp.dot(vT_ref[HD:2 * HD, pl.ds(start, wk)], p_w.astype(BF16), preferred_element_type=F32)

    pg = p_c[:, 0:qb]
    for h in range(1, NH):
        pg = pg + p_c[:, qb * h:qb * (h + 1)]
    pg_scr[0:8, :] = jnp.zeros((8, qb), F32)
    pg_scr[8:8 + nc, :] = pg
    pg_scr[8 + nc:16 + nc, :] = jnp.zeros((8, qb), F32)
    ns = nc // 4

    def strided(k):
        return pg_scr[pl.ds(8 + k, ns, stride=4), :]

    p_sel = 0.5 * strided(-1) + strided(0)
    p_sel = p_sel + strided(1)
    p_sel = p_sel + strided(2)
    p_sel = p_sel + 0.5 * strided(3)

    j_io = lax.broadcasted_iota(I32, (ns, qb), 0)
    j_f = j_io.astype(F32)
    cur = (q0 + lax.broadcasted_iota(I32, (ns, qb), 1)) >> SEL_SHIFT
    validb = j_io <= cur
    forced = validb & ((j_io == 0) | (j_io > cur - SEL_LOCAL))
    n_forced = 1 + SEL_LOCAL
    assert n_sel > n_forced
    open0 = validb & jnp.logical_not(forced)
    score = jnp.where(open0, p_sel, -jnp.inf)
    for _ in range(n_sel - n_forced):
        mx = jnp.max(score, axis=0, keepdims=True)
        first = jnp.min(jnp.where(score == mx, j_f, float(ns)), axis=0, keepdims=True)
        score = jnp.where(j_f == first, -jnp.inf, score)
    picked = forced | (open0 & (score == -jnp.inf))
    older = j_io < (q0 >> SEL_SHIFT)
    bias_scr[...] = jnp.where(picked & older, 0.0, NEG)

    assert tk == half * SEL_BLOCK

    def sel_scores(t, stab):
        kt = ks_ref[pl.ds(pl.multiple_of(t * tk, tk), tk), :]
        bt = bias_scr[pl.ds(pl.multiple_of(t * half, half), half), :]
        extra = jnp.concatenate([jnp.concatenate([bt] * NH, axis=1), stab], axis=0)
        return jnp.dot(kt, q_aug(extra), preferred_element_type=F32)

    def v_slc(t):
        return vT_ref[0:HD, pl.ds(pl.multiple_of(t * tk, tk), tk)]

    n_tiles = q0 // tk + 1
    k_own = ks_ref[pl.ds(pl.multiple_of(q0, qb), qb), :]
    v_own = vT_ref[0:HD, pl.ds(pl.multiple_of(q0, qb), qb)]
    causal = (lax.broadcasted_iota(I32, (qb, nq), 0)
              <= (lax.broadcasted_iota(I32, (qb, nq), 1) & (qb - 1)))

    def own_scores(stab):
        return jnp.where(causal, jnp.dot(k_own, q_aug(jnp.concatenate([zero_half, stab], axis=0)),
                                         preferred_element_type=F32), NEG)

    def sel_fast():
        stab = stab_rows(b_s)

        def step(l, acc, s, vt):
            p = jnp.exp2(s)
            return (l + jnp.sum(p, axis=0, keepdims=True),
                    acc + jnp.dot(vt, p.astype(BF16), preferred_element_type=F32))

        l, acc = lax.fori_loop(0, n_tiles, lambda t, c: step(c[0], c[1], sel_scores(t, stab), v_slc(t)),
                               (jnp.zeros((1, nq), F32), jnp.zeros((HD, nq), F32)))
        l, acc = step(l, acc, own_scores(stab), v_own)
        return acc / l

    def sel_slow():
        init = (jnp.full((1, nq), 2 * NEG, F32), jnp.zeros((1, nq), F32), jnp.zeros((HD, nq), F32))
        carry = lax.fori_loop(0, n_tiles, lambda t, c: _flash_cols(c, sel_scores(t, zero_half), v_slc(t)), init)
        _, l, acc = _flash_cols(carry, own_scores(zero_half), v_own)
        return acc / l

    def finish():
        o_s = lax.cond(fast, sel_fast, sel_slow)
        heads = []
        for h in range(NH):
            sl = slice(qb * h, qb * (h + 1))
            heads.append(g_t[3 * h:3 * h + 1, :] * o_c[:, sl] + g_t[3 * h + 1:3 * h + 2, :] * o_s[:, sl]
                         + g_t[3 * h + 2:3 * h + 3, :] * o_w[:, sl])
        o_rows = jnp.concatenate(heads, axis=0).T
        write_o(_silu(dg) * o_rows)

    return finish


def _attn(qT, ks, kw, vT, kc, vcT, gT, pd, kmax):
    s = ks.shape[0]
    n_sub = 2
    qb = LANES * n_sub
    nc = kc.shape[0]
    tk = (AUG // 2) * SEL_BLOCK
    kern = functools.partial(_attn_kernel, n_sel=min(SEL_TOPN, s // SEL_BLOCK), tk=tk, n_sub=n_sub)
    return _call(
        kern,
        grid=(s // qb,),
        in_specs=[pl.BlockSpec((WG, qb), lambda i: (0, i)),
                  _full(ks.shape), _full(kw.shape), _full(vT.shape), _full(kc.shape), _full(vcT.shape),
                  pl.BlockSpec((16, qb), lambda i: (0, i)),
                  pl.BlockSpec((qb, WG), lambda i: (i, O_DG // WG)),
                  _full(kmax.shape)],
        out_specs=pl.BlockSpec((qb, WG), lambda i: (i, 0)),
        out_shape=jax.ShapeDtypeStruct((s, WG), F32),
        scratch_shapes=[pltpu.VMEM((n_sub, nc + 16, LANES), F32), pltpu.VMEM((n_sub, nc // 4, LANES), F32)],
        compiler_params=_cparams(("arbitrary",), VMEM_BIG),
        name="attn",
    )(qT, ks, kw, vT, kc, vcT, gT, pd, kmax)


def _outproj_kernel(x_ref, yabc_ref, yd_ref, ada_ref, w_ref, o_ref):
    y = jnp.concatenate([yabc_ref[...], yd_ref[...]], axis=1)
    o_ref[...] = x_ref[...] + ada_ref[:, 2 * D_MODEL:3 * D_MODEL] * _bdot(y, w_ref[...])


def _outproj(x, yabc, yd, ada_rows, w):
    s = x.shape[0]
    tr = min(512, s)
    ar = ada_rows.shape[0]
    ada_spec = (_full((1, 3 * D_MODEL)) if ar == 1 else pl.BlockSpec((tr, 3 * D_MODEL), lambda i: (i, 0)))
    return _call(
        _outproj_kernel,
        grid=(s // tr,),
        in_specs=[pl.BlockSpec((tr, D_MODEL), lambda i: (i, 0)),
                  pl.BlockSpec((tr, 3 * WG), lambda i: (i, 0)),
                  pl.BlockSpec((tr, WG), lambda i: (i, 0)),
                  ada_spec, _full((4 * WG, D_MODEL))],
        out_specs=pl.BlockSpec((tr, D_MODEL), lambda i: (i, 0)),
        out_shape=jax.ShapeDtypeStruct((s, D_MODEL), F32),
        compiler_params=_cparams(("arbitrary",)),
        name="outproj",
    )(x, yabc, yd, ada_rows, w)


def _pad_t(scr, x):
    nb = x.shape[0]
    scr[...] = jnp.zeros_like(scr)
    scr[0:nb, :] = x
    return scr[...].T


def _spre_kernel(x_ref, ada_ref, g_ref, w_ref, sc_ref, bc_ref, hs_ref, cf_ref,
                 scw_ref, bcw_ref, bcb_ref, dtb_ref, alog_ref, dvec_ref, ng_ref,
                 cfw_ref, cfb_ref, lng_ref, lnb_ref, qg_ref, kg_ref,
                 y_ref, q_ref, rows_ref, win_ref, gates_ref, sgd_ref,
                 sco_ref, bco_ref, hso_ref, cfo_ref,
                 xbc_scr, padA, padB):
    nb = x_ref.shape[0]
    h = _modulated_norm(x_ref[...], g_ref[...], ada_ref[:, D_MODEL:2 * D_MODEL], ada_ref[:, 0:D_MODEL])
    p = _bdot(h, w_ref[...])

    u_a = p[:, 512:768] * p[:, 0:256]
    conv_a = scw_ref[0:1, :] * sc_ref[0] + scw_ref[1:2, :] * sc_ref[1] + scw_ref[2:3, :] * u_a
    y_ref[:, 0:WG] = _silu(p[:, 768:1024]) * p[:, 256:512] * conv_a
    sco_ref[0] = sc_ref[1]
    sco_ref[1] = u_a

    bx = p[:, O_BX:O_BX + SSD_CD]
    conv_b = (bcw_ref[0:1, :] * bc_ref[0] + bcw_ref[1:2, :] * bc_ref[1] + bcw_ref[2:3, :] * bc_ref[2]
              + bcw_ref[3:4, :] * bx + bcb_ref[...])
    bco_ref[0] = bc_ref[1]
    bco_ref[1] = bc_ref[2]
    bco_ref[2] = bx
    xbc = _silu(conv_b)
    xbc_scr[...] = xbc
    xs = xbc[:, 0:WG]
    dtp = _softplus(p[:, O_BDT:O_BDT + LANES] + dtb_ref[...])
    dt_c = jnp.concatenate([jnp.broadcast_to(dtp[:, hh:hh + 1], (nb, SSD_P)) for hh in range(SSD_H)], axis=1)
    dec_t = _pad_t(padA, jnp.exp(dt_c * (-jnp.exp(alog_ref[...]))))
    xdt_t = _pad_t(padB, xs * dt_c)
    lane = lax.broadcasted_iota(I32, (SSD_H * SSD_P, LANES), 1)

    def sample_step(n, ymat):
        oh = lane == n
        dcol = jnp.sum(jnp.where(oh, dec_t, 0.0), axis=1, keepdims=True)
        xcol = jnp.sum(jnp.where(oh, xdt_t, 0.0), axis=1, keepdims=True)
        brow = xbc_scr[pl.ds(n, 1), WG:WG + 2 * SSD_S]
        crow = xbc_scr[pl.ds(n, 1), WG + 2 * SSD_S:WG + 4 * SSD_S]
        half = 2 * SSD_P
        bfull = jnp.concatenate([jnp.broadcast_to(brow[:, 0:SSD_S], (half, SSD_S)),
                                 jnp.broadcast_to(brow[:, SSD_S:], (half, SSD_S))], axis=0)
        cfull = jnp.concatenate([jnp.broadcast_to(crow[:, 0:SSD_S], (half, SSD_S)),
                                 jnp.broadcast_to(crow[:, SSD_S:], (half, SSD_S))], axis=0)
        hn = dcol * hs_ref[n] + xcol * bfull
        hso_ref[n] = hn
        ycol = jnp.sum(hn * cfull, axis=1, keepdims=True)
        return jnp.where(oh, ycol, ymat)

    ymat = lax.fori_loop(0, nb, sample_step, jnp.zeros((SSD_H * SSD_P, LANES), F32))
    y_b = ymat.T[0:nb, :] + dvec_ref[...] * xs
    y_ref[:, WG:2 * WG] = _rms(y_b * _silu(p[:, O_BZ:O_BZ + WG])) * ng_ref[...]

    glu = p[:, O_CGLU:O_CGLU + WG] * jax.nn.sigmoid(p[:, O_CGLU + WG:O_CGLU + 2 * WG])
    u_c = cfw_ref[CF_K - 1:CF_K, :] * glu + cfb_ref[...]
    for k in range(CF_K - 1):
        u_c = u_c + cfw_ref[k:k + 1, :] * cf_ref[k]
    for k in range(CF_K - 2):
        cfo_ref[k] = cf_ref[k + 1]
    cfo_ref[CF_K - 2] = glu
    mu = jnp.mean(u_c, axis=-1, keepdims=True)
    var = jnp.mean(jnp.square(u_c - mu), axis=-1, keepdims=True)
    ln = (u_c - mu) * lax.rsqrt(var + EPS) * lng_ref[...] + lnb_ref[...]
    y_ref[:, 2 * WG:3 * WG] = _silu(p[:, O_CG:O_CG + WG]) * _silu(ln)

    pd = p[:, N_ABC:]
    q, rows, win, gates = _nsa_rows(pd, qg_ref[...], kg_ref[1:2, :], kg_ref[2:3, :])
    q_ref[...] = q
    rows_ref[...] = rows
    win_ref[...] = win
    gates_ref[...] = gates
    sgd_ref[...] = _silu(pd[:, O_DG:O_DG + WG])


def _spre(x, ada_rows, g, w, st, wts):
    nb = x.shape[0]
    small = [wts[k] for k in ("sc_w", "bc_w", "bc_b", "dtb128", "alog256", "dvec", "ssd_ng",
                              "cf_w", "cf_b", "ln_g", "ln_b", "q_g", "k_g")]
    ins = [x, ada_rows, g, w, st["sc"], st["bc"], st["hs"], st["cf"]] + small
    outs = [jax.ShapeDtypeStruct((nb, 3 * WG), F32),
            jax.ShapeDtypeStruct((nb, WG), F32),
            jax.ShapeDtypeStruct((nb, WG), F32),
            jax.ShapeDtypeStruct((nb, 2 * HD), F32),
            jax.ShapeDtypeStruct((nb, LANES), F32),
            jax.ShapeDtypeStruct((nb, WG), F32),
            jax.ShapeDtypeStruct(st["sc"].shape, F32),
            jax.ShapeDtypeStruct(st["bc"].shape, F32),
            jax.ShapeDtypeStruct(st["hs"].shape, F32),
            jax.ShapeDtypeStruct(st["cf"].shape, F32)]
    return _call(
        _spre_kernel,
        grid=(1,),
        in_specs=[_full(a.shape) for a in ins],
        out_specs=[_full(o.shape) for o in outs],
        out_shape=outs,
        scratch_shapes=[pltpu.VMEM((nb, SSD_CD), F32), pltpu.VMEM((LANES, WG), F32),
                        pltpu.VMEM((LANES, WG), F32)],
        compiler_params=_cparams(("arbitrary",), VMEM_BIG),
        name="sample_pre",
    )(*ins)


def _scmp_kernel(pt_ref, *refs, pg, n_groups, layer):
    del pt_ref, layer
    pages = refs[:pg]
    (perm_ref, wbd_ref, pe_ref, b1_ref, w2_ref, b2_ref, kg_ref, q_ref, imp_ref,
     oc_ref, psel_ref, permscr, pre) = refs[pg:]
    g = pl.program_id(1)
    cpp = PAGE // CMP_STRIDE
    for p in range(pg):
        both = pages[p][0, 0].reshape(2 * HD, PAGE)
        kp = _bdot_nt(perm_ref[...], both)
        for j in range(CMP_STRIDE):
            permscr[j, cpp * p:cpp * (p + 1), :] = kp[cpp * j:cpp * (j + 1), :]
    acc = None
    for j in range(CMP_STRIDE):
        term = _bdot(permscr[j], wbd_ref[j])
        acc = term if acc is None else acc + term
    rows = cpp * pg
    pre[pl.ds(pl.multiple_of(g * rows, rows), rows), :] = acc

    @pl.when(g == n_groups - 1)
    def _():
        nc = rows * n_groups
        pre[nc:nc + 8, :] = jnp.zeros((8, pre.shape[1]), F32)
        kc, vc = _cmp_tail(pre, nc, pe_ref, b1_ref, w2_ref, b2_ref, kg_ref[0:1, :])
        s = _bdot_nt(q_ref[0], kc)
        c_io = lax.broadcasted_iota(I32, (8, nc), 1)
        valid = c_io <= nc - 2
        sm = jnp.where(valid, s, NEG)
        m = jnp.max(sm, axis=1, keepdims=True)
        e = jnp.where(valid, jnp.exp2(sm - m), 0.0)
        p = e / jnp.sum(e, axis=1, keepdims=True)
        oc_ref[0] = _bdot(p, vc)
        ps8 = _dot_exact_rhs(p, imp_ref[...])
        psel_ref[0] = ps8[0:1, :] + ps8[1:2, :] + ps8[2:3, :] + ps8[3:4, :]


def _scmp(ckv_t, page_table, layer, q8, wts, consts):
    nb, n_pages = page_table.shape
    pg = min(16, n_pages)
    n_groups = n_pages // pg
    nc = n_pages * (PAGE // CMP_STRIDE)
    hid4 = 4 * wts["cmp_b1"].shape[-1]
    nsp = consts["imp"].shape[1]

    def page_spec(k):
        return pl.BlockSpec((1, 1, 2, HD, PAGE), lambda n, g, pt: (layer, pt[n, g * pg + k], 0, 0, 0))

    ws = [consts["perm"], wts["wbd"], wts["pe"], wts["cmp_b1"], wts["cmp_w2"], wts["cmp_b2"], wts["k_g"]]
    in_specs = ([page_spec(k) for k in range(pg)]
                + [pl.BlockSpec(w.shape, functools.partial(lambda nd, n, g, pt: (0,) * nd, w.ndim)) for w in ws]
                + [pl.BlockSpec((1, 8, HD), lambda n, g, pt: (n, 0, 0)),
                   pl.BlockSpec(consts["imp"].shape, lambda n, g, pt: (0, 0))])
    kern = functools.partial(_scmp_kernel, pg=pg, n_groups=n_groups, layer=layer)
    return pl.pallas_call(
        kern,
        grid_spec=pltpu.PrefetchScalarGridSpec(
            num_scalar_prefetch=1,
            grid=(nb, n_groups),
            in_specs=in_specs,
            out_specs=[pl.BlockSpec((1, 8, HD), lambda n, g, pt: (n, 0, 0)),
                       pl.BlockSpec((1, 1, nsp), lambda n, g, pt: (n, 0, 0))],
            scratch_shapes=[pltpu.VMEM((CMP_STRIDE, pg * PAGE // CMP_STRIDE, 2 * HD), F32),
                            pltpu.VMEM((nc + 8, hid4), F32)],
        ),
        out_shape=[jax.ShapeDtypeStruct((nb, 8, HD), F32), jax.ShapeDtypeStruct((nb, 1, nsp), F32)],
        compiler_params=_cparams(("arbitrary", "arbitrary"), VMEM_BIG),
        name="sample_cmp",
    )(page_table, *([ckv_t] * pg), *ws, q8, consts["imp"])


def _stopk_kernel(p_ref, o_ref, *, cur, n_sel):
    p = p_ref[...]
    j_io = lax.broadcasted_iota(I32, p.shape, 1)
    j_f = j_io.astype(F32)
    validb = j_io <= cur
    forced = validb & ((j_io == 0) | (j_io > cur - 2))
    score = jnp.where(forced, jnp.inf, jnp.where(validb, p, -jnp.inf))
    selm = jnp.zeros(p.shape, F32)
    out = jnp.zeros(o_ref.shape, I32)
    o_io = lax.broadcasted_iota(I32, o_ref.shape, 1)
    for k in range(n_sel):
        free = selm == 0.0
        mx = jnp.max(jnp.where(free, score, -jnp.inf), axis=1, keepdims=True)
        hit = free & (score == mx)
        first = jnp.min(jnp.where(hit, j_f, float(p.shape[1])), axis=1, keepdims=True)
        selm = jnp.where(j_f == first, 1.0, selm)
        out = jnp.where(o_io == k, first.astype(I32), out)
    o_ref[...] = out


def _stopk(psel, cur, n_sel):
    nb = psel.shape[0]
    return _call(
        functools.partial(_stopk_kernel, cur=cur, n_sel=n_sel),
        grid=(1,),
        in_specs=[_full(psel.shape)],
        out_specs=_full((nb, LANES)),
        out_shape=jax.ShapeDtypeStruct((nb, LANES), I32),
        name="sample_topk",
    )(psel)


def _flash_rows(carry, s, vt):
    m, l, acc = carry
    mn = jnp.maximum(m, jnp.max(s, axis=1, keepdims=True))
    alpha = jnp.exp2(m - mn)
    p = jnp.exp2(s - mn)
    l = alpha * l + jnp.sum(p, axis=1, keepdims=True)
    acc = alpha * acc + _bdot_nt(p, vt)
    return mn, l, acc


def _bf(x):
    return x.astype(BF16).astype(F32)


def _row_to_col(row):
    n = row.shape[1]
    diag = lax.broadcasted_iota(I32, (n, n), 0) == lax.broadcasted_iota(I32, (n, n), 1)
    return jnp.sum(jnp.where(diag, jnp.broadcast_to(row, (n, n)), 0.0), axis=1, keepdims=True)


def _ssel_kernel(idx_ref, pt_ref, *refs, n_sel, cur, layer):
    del pt_ref, layer
    pages = refs[:n_sel]
    cw_ref, q_ref, new_ref, g_ref, oc_ref, o_ref, wo_ref = refs[n_sel:]
    n = pl.program_id(0)
    q8 = q_ref[0]
    new = new_ref[0]
    lane_half = lax.broadcasted_iota(I32, (8, PAGE), 1) >> SEL_SHIFT

    carry = (jnp.full((8, 1), 2 * NEG, F32), jnp.zeros((8, 1), F32), jnp.zeros((8, HD), F32))
    for k in range(n_sel):
        blk = idx_ref[n, k]
        s = _bdot(q8, pages[k][0, 0, 0])
        want = jnp.where(blk < cur, blk & 1, 2)
        carry = _flash_rows(carry, jnp.where(lane_half == want, s, NEG), pages[k][0, 0, 1])
    m, l, acc = carry
    s_new = jnp.sum(_bf(q8) * _bf(new[0:1, :]), axis=1, keepdims=True)
    mn = jnp.maximum(m, s_new)
    alpha = jnp.exp2(m - mn)
    p_new = jnp.exp2(s_new - mn)
    o_s = (alpha * acc + _bf(p_new) * _bf(new[1:2, :])) / (alpha * l + p_new)

    kw = cw_ref[0, 0, 0]
    vw = cw_ref[0, 0, 1]
    s_w = _bdot(q8, kw)
    s_wn = jnp.sum(_bf(q8) * _bf(new[2:3, :]), axis=1, keepdims=True)
    mw = jnp.maximum(jnp.max(s_w, axis=1, keepdims=True), s_wn)
    e = jnp.exp2(s_w - mw)
    e_n = jnp.exp2(s_wn - mw)
    o_w = (_bdot_nt(e, vw) + _bf(e_n) * _bf(new[3:4, :])) / (jnp.sum(e, axis=1, keepdims=True) + e_n)

    g = g_ref[0]
    o_ref[0] = g[:, 0:1] * oc_ref[0] + g[:, 1:2] * o_s + g[:, 2:3] * o_w

    wb = kw.shape[1]
    last = lax.broadcasted_iota(I32, (HD, wb), 1) == wb - 1
    wo_ref[0, 0, 0] = jnp.where(last, _row_to_col(new[2:3, :]), pltpu.roll(kw, wb - 1, axis=1))
    wo_ref[0, 0, 1] = jnp.where(last, _row_to_col(new[3:4, :]), pltpu.roll(vw, wb - 1, axis=1))


def _ssel(ckv_t, cw_t, idx, page_table, layer, q8, new8, g8, oc):
    nb, n_pages = page_table.shape
    n_sel = idx.shape[1]
    wb = cw_t.shape[-1]
    cur = n_pages * (PAGE // SEL_BLOCK)

    def page_spec(k):
        def imap(n, ix, pt):
            return (layer, pt[n, jnp.minimum(ix[n, k] // 2, n_pages - 1)], 1, 0, 0)
        return pl.BlockSpec((1, 1, 2, HD, PAGE), imap)

    in_specs = ([page_spec(k) for k in range(n_sel)]
                + [pl.BlockSpec((1, 1, 2, HD, wb), lambda n, ix, pt: (layer, n, 0, 0, 0)),
                   pl.BlockSpec((1, 8, HD), lambda n, ix, pt: (n, 0, 0)),
                   pl.BlockSpec((1, 8, HD), lambda n, ix, pt: (n, 0, 0)),
                   pl.BlockSpec((1, 8, 3), lambda n, ix, pt: (n, 0, 0)),
                   pl.BlockSpec((1, 8, HD), lambda n, ix, pt: (n, 0, 0))])
    kern = functools.partial(_ssel_kernel, n_sel=n_sel, cur=cur, layer=layer)
    return pl.pallas_call(
        kern,
        grid_spec=pltpu.PrefetchScalarGridSpec(
            num_scalar_prefetch=2,
            grid=(nb,),
            in_specs=in_specs,
            out_specs=[pl.BlockSpec((1, 8, HD), lambda n, ix, pt: (n, 0, 0)),
                       pl.BlockSpec((1, 1, 2, HD, wb), lambda n, ix, pt: (0, n, 0, 0, 0))],
        ),
        out_shape=[jax.ShapeDtypeStruct((nb, 8, HD), F32),
                   jax.ShapeDtypeStruct((1, nb, 2, HD, wb), F32)],
        compiler_params=_cparams(("arbitrary",)),
        name="sample_sel",
    )(idx, page_table, *([ckv_t] * n_sel), cw_t, q8, new8, g8, oc)


def _spost_kernel(x_ref, yabc_ref, sgd_ref, o_ref, ada_ref, w_ref, out_ref):
    y = jnp.concatenate([yabc_ref[...], sgd_ref[...] * o_ref[...]], axis=1)
    out_ref[...] = x_ref[...] + ada_ref[:, 2 * D_MODEL:3 * D_MODEL] * _bdot(y, w_ref[...])


def _spost(x, yabc, sgd, o, ada_rows, w):
    ins = [x, yabc, sgd, o, ada_rows, w]
    return _call(
        _spost_kernel,
        grid=(1,),
        in_specs=[_full(a.shape) for a in ins],
        out_specs=_full(x.shape),
        out_shape=jax.ShapeDtypeStruct(x.shape, F32),
        name="sample_post",
    )(*ins)


def _pad_cols(w, n):
    return jnp.concatenate([w, jnp.zeros(w.shape[:-1] + (n - w.shape[-1],), w.dtype)], axis=-1)


def _prep_w_in(w):
    o = np.cumsum([0, 256, 256, 256, 256, 256, 768, 4, 512, 256, 256, 384, 256, 12]).tolist()
    sec = [w[:, o[k]:o[k + 1]] for k in range(13)]
    a_h, a_b, a_c, a_g, b_z, b_xbc, b_dt, c_glu, c_g, d_q, d_kv, d_g, d_bg = sec
    cols = [a_h, a_b, a_c, a_g, b_z, b_xbc, _pad_cols(b_dt, LANES), c_glu, c_g,
            d_q, d_g, d_kv, _pad_cols(d_bg, LANES)]
    return jnp.concatenate(cols, axis=1).astype(BF16)


def _prep_wbd(w1):
    hid = w1.shape[-1]
    z = jnp.zeros((CMP_STRIDE, HD, hid), w1.dtype)
    top = jnp.concatenate([w1[0, :CMP_STRIDE], w1[0, CMP_STRIDE:], z, z], axis=-1)
    bot = jnp.concatenate([z, z, w1[1, :CMP_STRIDE], w1[1, CMP_STRIDE:]], axis=-1)
    return jnp.concatenate([top, bot], axis=1).astype(BF16)


def _layer_weights(l, a, pe_all):
    row = lambda v: v.reshape(1, -1)
    return {
        "sc_w": a["sc_w"][l], "bc_w": a["ssd_conv_w"][l], "bc_b": row(a["ssd_conv_b"][l]),
        "dtb128": _pad_cols(row(a["ssd_dt_bias"][l]), LANES),
        "alog128": _pad_cols(row(a["ssd_A_log"][l]), LANES),
        "alog256": row(jnp.repeat(a["ssd_A_log"][l], SSD_P)),
        "dvec": row(jnp.repeat(a["ssd_D"][l], SSD_P)),
        "ssd_ng": row(a["ssd_norm_g"][l]),
        "cf_w": a["cf_w"][l], "cf_b": row(a["cf_b"][l]),
        "ln_g": row(a["cf_ln_g"][l]), "ln_b": row(a["cf_ln_b"][l]),
        "q_g": row(jnp.tile(a["q_g"][l], NH)), "k_g": a["k_g"][l],
        "wbd": _prep_wbd(a["cmp_w1"][l]), "pe": pe_all[2 * l:2 * l + 2],
        "cmp_b1": a["cmp_b1"][l].reshape(2, 1, -1), "cmp_w2": a["cmp_w2"][l],
        "cmp_b2": a["cmp_b2"][l].reshape(2, 1, -1),
    }


def _sample_consts(n_pages):
    cpp = PAGE // CMP_STRIDE
    perm = np.zeros((PAGE, PAGE), np.float32)
    for j in range(CMP_STRIDE):
        for m in range(cpp):
            perm[cpp * j + m, CMP_STRIDE * m + j] = 1.0
    nc = n_pages * cpp
    ns = n_pages * (PAGE // SEL_BLOCK) + 1
    nsp = -(-ns // LANES) * LANES
    imp = np.zeros((nc, nsp), np.float32)
    r = SEL_BLOCK // CMP_STRIDE
    for j in range(ns):
        for o, wgt in ((-1, 0.5), (0, 1.0), (1, 1.0), (2, 1.0), (3, 0.5)):
            c = r * j + o
            if 0 <= c < nc:
                imp[c, j] = wgt
    return {"perm": jnp.asarray(perm, BF16), "imp": jnp.asarray(imp, BF16)}


def kernel(x_prompt, x_sample, cache_kv, cache_win, state_sconv, state_ssm_conv, state_ssm, state_cconv, page_table, c_prompt, c_sample, norm_g, w_ada, b_ada, w_in, w_out, sc_w, ssd_conv_w, ssd_conv_b, ssd_dt_bias, ssd_A_log, ssd_D, ssd_norm_g, cf_w, cf_b, cf_ln_g, cf_ln_b, q_g, k_g, cmp_pe, cmp_w1, cmp_b1, cmp_w2, cmp_b2):
    a = dict(sc_w=sc_w, ssd_conv_w=ssd_conv_w, ssd_conv_b=ssd_conv_b, ssd_dt_bias=ssd_dt_bias,
             ssd_A_log=ssd_A_log, ssd_D=ssd_D, ssd_norm_g=ssd_norm_g, cf_w=cf_w, cf_b=cf_b,
             cf_ln_g=cf_ln_g, cf_ln_b=cf_ln_b, q_g=q_g, k_g=k_g, cmp_w1=cmp_w1, cmp_b1=cmp_b1,
             cmp_w2=cmp_w2, cmp_b2=cmp_b2)
    depth = w_in.shape[0]
    assert x_prompt.shape[0] == 1 and x_sample.shape[1] == 1
    s = x_prompt.shape[1]
    nb = x_sample.shape[0]
    n_pages = page_table.shape[1]
    past = n_pages * PAGE
    assert s % (AUG // 2 * SEL_BLOCK) == 0 and s >= WINDOW + LANES and cache_win.shape[2] == min(WINDOW, past)

    n_c = 1 + nb
    c_all = jnp.concatenate([c_prompt, c_sample, jnp.zeros((-n_c % 8, D_MODEL), F32)], axis=0)
    ada = _ada(c_all, w_ada, b_ada)
    pe_all = _pe_terms(cmp_pe, cmp_w1)

    ckv_t = jnp.transpose(cache_kv, (0, 1, 3, 4, 2))
    cw_t = jnp.transpose(cache_win, (0, 1, 3, 4, 2))
    consts = _sample_consts(n_pages)

    xp = x_prompt[0]
    xs = x_sample[:, 0]
    outs_p, outs_s = [], []
    for l in range(depth):
        wts = _layer_weights(l, a, pe_all)
        w_in_l = _prep_w_in(w_in[l])
        w_out_l = w_out[l].astype(BF16)
        g_l = norm_g[l].reshape(1, -1)

        ada_p = ada[l, 0:1]
        pabc, pd = _inproj(xp, ada_p, g_l, w_in_l)
        yabc, sc_p, bc_p, hs_p, cf_p = _mix(pabc, wts)
        qT, ks, kw, vT, kvp, winT, gT, kmax = _nsaprep(pd, wts["q_g"], wts["k_g"])
        kc, vcT = _pcompress(pd, wts)
        yd = _attn(qT, ks, kw, vT, kc, vcT, gT, pd, kmax)
        xp = _outproj(xp, yabc, yd, ada_p, w_out_l)
        wlen = min(WINDOW, s)
        outs_p.append((jnp.transpose(kvp, (0, 3, 1, 2)),
                       jnp.transpose(winT[:, :, s - wlen:], (2, 0, 1))[None],
                       sc_p[None], bc_p[None], hs_p.reshape(1, SSD_H, SSD_P, SSD_S), cf_p[None]))

        ada_s = ada[l, 1:1 + nb]
        st = {"sc": jnp.transpose(state_sconv[l], (1, 0, 2)), "bc": jnp.transpose(state_ssm_conv[l], (1, 0, 2)),
              "hs": state_ssm[l].reshape(nb, SSD_H * SSD_P, SSD_S), "cf": jnp.transpose(state_cconv[l], (1, 0, 2))}
        (yabc_s, q_s, rows_s, win_s, gates_s, sgd_s, sc_s, bc_s, hs_s, cf_s) = _spre(xs, ada_s, g_l, w_in_l, st, wts)
        pad8 = lambda v: jnp.concatenate([v, jnp.zeros_like(v)], axis=1)
        q8 = pad8(q_s.reshape(nb, NH, HD))
        oc, psel = _scmp(ckv_t, page_table, l, q8, wts, consts)
        ns = past // SEL_BLOCK + 1
        idx = _stopk(psel.reshape(nb, -1), ns - 1, min(SEL_TOPN, ns))[:, :min(SEL_TOPN, ns)]
        new8 = pad8(jnp.concatenate([rows_s[:, 2 * HD:4 * HD], win_s], axis=1).reshape(nb, 4, HD))
        g8 = pad8(gates_s[:, 0:3 * NH].reshape(nb, NH, 3))
        o_s, win_new = _ssel(ckv_t, cw_t, idx, page_table, l, q8, new8, g8, oc)
        xs = _spost(xs, yabc_s, sgd_s, o_s[:, 0:NH].reshape(nb, WG), ada_s, w_out_l)
        outs_s.append((rows_s.reshape(nb, 1, 4, HD),
                       jnp.transpose(win_new[0], (0, 3, 1, 2)),
                       jnp.transpose(sc_s, (1, 0, 2)), jnp.transpose(bc_s, (1, 0, 2)),
                       hs_s.reshape(nb, SSD_H, SSD_P, SSD_S), jnp.transpose(cf_s, (1, 0, 2))))

    kv_p, win_p, sc_p, bc_p, h_p, cf_p = [jnp.stack(v) for v in zip(*outs_p)]
    kv_s, win_s, sc_s, bc_s, h_s, cf_s = [jnp.stack(v) for v in zip(*outs_s)]
    return (xp[None], xs[:, None], kv_p, kv_s, win_p, win_s, sc_p, sc_s, bc_p, bc_s, h_p, h_s, cf_p, cf_s)
```
